```python
import math
import jax, jax.numpy as jnp
from jax import lax
import numpy as np

D_MODEL = 1024
BATCH = 2
SEQ = 16384
DEPTH = 1

PLE_DIM = 256
SB_HEADS = 8
SB_HEAD_DIM = 64
SB_WIDTH = SB_HEADS * SB_HEAD_DIM
SB_BLOCK = 128
SSM_WIDTH = D_MODEL // 2
SSM_GROUP = 16
SSM_GROUPS = SSM_WIDTH // SSM_GROUP
SSM_STATE = 64
D_FF = -(-8 * D_MODEL // (3 * 256)) * 256
IN_COLS = 3 * SB_WIDTH + SSM_WIDTH + 2 * D_MODEL
DEEPNORM_ALPHA = (2 * DEPTH) ** 0.25
DEEPNORM_BETA = (8 * DEPTH) ** -0.25
LN_EPS = 1e-5
DT_MIN = 1e-3
DT_MAX = 1e-1

kernel_name = 'hybrid_stickbreak_s5_deepnorm_block'


def layer_norm(x, g, b):
    xf = x.astype(jnp.float32)
    mu = jnp.mean(xf, axis=-1, keepdims=True)
    var = jnp.mean(jnp.square(xf - mu), axis=-1, keepdims=True)
    y = (xf - mu) * lax.rsqrt(var + LN_EPS)
    return (y * g.astype(jnp.float32) + b.astype(jnp.float32)).astype(x.dtype)


def stick_breaking_attention(q, k, v):
    bsz, seq, _ = q.shape

    def heads(t):
        return t.reshape(bsz, seq, SB_HEADS, SB_HEAD_DIM).transpose(0, 2, 1, 3)

    q, k, v = heads(q), heads(k), heads(v)
    scale = SB_HEAD_DIM ** -0.5
    offs = jnp.arange(SB_BLOCK, dtype=jnp.int32)

    def query_block(qi):
        q_blk = lax.dynamic_slice_in_dim(q, qi * SB_BLOCK, SB_BLOCK, axis=2)
        t_pos = qi * SB_BLOCK + offs

        def key_block(n, carry):
            acc, log_surv = carry
            kj = qi - n
            k_blk = lax.dynamic_slice_in_dim(k, kj * SB_BLOCK, SB_BLOCK, axis=2)
            v_blk = lax.dynamic_slice_in_dim(v, kj * SB_BLOCK, SB_BLOCK, axis=2)
            s_pos = kj * SB_BLOCK + offs
            causal = s_pos[None, :] < t_pos[:, None]
            z = jnp.einsum('bhqd,bhkd->bhqk', q_blk, k_blk).astype(jnp.float32) * scale
            log_beta = jax.nn.log_sigmoid(z)
            log_1mb = jnp.where(causal, jax.nn.log_sigmoid(-z), 0.0)
            incl = jnp.cumsum(log_1mb, axis=-1)
            total = incl[..., -1]
            log_w = log_beta + (total[..., None] - incl) + log_surv[..., None]
            w = jnp.where(causal, jnp.exp(log_w), 0.0)
            acc = acc + jnp.einsum('bhqk,bhkd->bhqd', w, v_blk.astype(jnp.float32))
            return acc, log_surv + total

        init = (jnp.zeros((bsz, SB_HEADS, SB_BLOCK, SB_HEAD_DIM), jnp.float32),
                jnp.zeros((bsz, SB_HEADS, SB_BLOCK), jnp.float32))
        acc, _ = lax.fori_loop(0, qi + 1, key_block, init)
        return acc

    n_blocks = seq // SB_BLOCK
    out = lax.map(query_block, jnp.arange(n_blocks, dtype=jnp.int32))
    out = out.transpose(1, 0, 3, 2, 4).reshape(bsz, seq, SB_WIDTH)
    return out.astype(v.dtype)


def s5_branch(u, lam_re, lam_im, log_dt, b_re, b_im, c_re, c_im, d_skip, glu_w, glu_b):
    bsz, seq, _ = u.shape
    f32 = jnp.float32
    uf = u.astype(f32).reshape(bsz, seq, SSM_GROUPS, SSM_GROUP)
    lam_re, lam_im = lam_re.astype(f32), lam_im.astype(f32)
    dt = jnp.exp(log_dt.astype(f32))[:, None]
    mag = jnp.exp(lam_re * dt)
    lb_re = mag * jnp.cos(lam_im * dt)
    lb_im = mag * jnp.sin(lam_im * dt)
    den = lam_re * lam_re + lam_im * lam_im
    n_re = lb_re - 1.0
    coef_re = (n_re * lam_re + lb_im * lam_im) / den
    coef_im = (lb_im * lam_re - n_re * lam_im) / den
    b_re, b_im = b_re.astype(f32), b_im.astype(f32)
    bb_re = coef_re[..., None] * b_re - coef_im[..., None] * b_im
    bb_im = coef_re[..., None] * b_im + coef_im[..., None] * b_re
    bu_re = jnp.einsum('bsgc,gpc->bsgp', uf, bb_re)
    bu_im = jnp.einsum('bsgc,gpc->bsgp', uf, bb_im)
    a_re = jnp.broadcast_to(lb_re, bu_re.shape)
    a_im = jnp.broadcast_to(lb_im, bu_im.shape)

    def combine(e1, e2):
        a1r, a1i, b1r, b1i = e1
        a2r, a2i, b2r, b2i = e2
        return (a2r * a1r - a2i * a1i,
                a2r * a1i + a2i * a1r,
                a2r * b1r - a2i * b1i + b2r,
                a2r * b1i + a2i * b1r + b2i)

    _, _, s_re, s_im = lax.associative_scan(combine, (a_re, a_im, bu_re, bu_im), axis=1)
    y = (jnp.einsum('bsgp,gcp->bsgc', s_re, c_re.astype(f32))
         - jnp.einsum('bsgp,gcp->bsgc', s_im, c_im.astype(f32)))
    y = y.reshape(bsz, seq, SSM_WIDTH) + d_skip.astype(f32) * uf.reshape(bsz, seq, SSM_WIDTH)
    y = jax.nn.gelu(y)
    y = y * jax.nn.sigmoid(y @ glu_w.astype(f32) + glu_b.astype(f32))
    return y.astype(u.dtype)


def setup_inputs(seed: int = 0) -> dict:
    key = jax.random.key(seed)
    ks = jax.random.split(key, 28)
    f32 = jnp.float32

    def nrm(k, shape, scale):
        return jax.random.normal(k, shape, f32) * scale

    x = jax.random.normal(ks[0], (BATCH, SEQ, D_MODEL), f32)
    p = jax.random.normal(ks[1], (DEPTH, BATCH, SEQ, PLE_DIM), f32)
    w_in = nrm(ks[2], (DEPTH, D_MODEL, IN_COLS), D_MODEL ** -0.5)
    ssm_lambda_re = -0.5 + nrm(ks[3], (DEPTH, SSM_GROUPS, SSM_STATE), 0.01)
    ssm_lambda_im = (math.pi * jnp.arange(SSM_STATE, dtype=f32))[None, None, :] + nrm(ks[4], (DEPTH, SSM_GROUPS, SSM_STATE), 0.01)
    ssm_log_dt = jax.random.uniform(ks[5], (DEPTH, SSM_GROUPS), f32, math.log(DT_MIN), math.log(DT_MAX))
    ssm_b_re = nrm(ks[6], (DEPTH, SSM_GROUPS, SSM_STATE, SSM_GROUP), (2 * SSM_GROUP) ** -0.5)
    ssm_b_im = nrm(ks[7], (DEPTH, SSM_GROUPS, SSM_STATE, SSM_GROUP), (2 * SSM_GROUP) ** -0.5)
    ssm_c_re = nrm(ks[8], (DEPTH, SSM_GROUPS, SSM_GROUP, SSM_STATE), (2 * SSM_STATE) ** -0.5)
    ssm_c_im = nrm(ks[9], (DEPTH, SSM_GROUPS, SSM_GROUP, SSM_STATE), (2 * SSM_STATE) ** -0.5)
    ssm_d = nrm(ks[10], (DEPTH, SSM_WIDTH), 1.0)
    ssm_glu_w = nrm(ks[11], (DEPTH, SSM_WIDTH, SSM_WIDTH), SSM_WIDTH ** -0.5)
    ssm_glu_b = nrm(ks[12], (DEPTH, SSM_WIDTH), 0.01)
    w_branch_a = nrm(ks[13], (DEPTH, SB_WIDTH, D_MODEL), SB_WIDTH ** -0.5)
    w_branch_b = nrm(ks[14], (DEPTH, SSM_WIDTH, D_MODEL), SSM_WIDTH ** -0.5)
    w_out = nrm(ks[15], (DEPTH, D_MODEL, D_MODEL), DEEPNORM_BETA * D_MODEL ** -0.5)
    ln1_g = 1.0 + nrm(ks[16], (DEPTH, D_MODEL), 0.02)
    ln1_b = nrm(ks[17], (DEPTH, D_MODEL), 0.02)
    ffn_w_gate = nrm(ks[18], (DEPTH, D_MODEL, D_FF), D_MODEL ** -0.5)
    ffn_w_up = nrm(ks[19], (DEPTH, D_MODEL, D_FF), D_MODEL ** -0.5)
    ffn_w_down = nrm(ks[20], (DEPTH, D_FF, D_MODEL), DEEPNORM_BETA * D_FF ** -0.5)
    ple_w_gate = nrm(ks[21], (DEPTH, D_MODEL, D_MODEL), D_MODEL ** -0.5)
    ple_w_proj = nrm(ks[22], (DEPTH, PLE_DIM, D_MODEL), DEEPNORM_BETA * PLE_DIM ** -0.5)
    ln2_g = 1.0 + nrm(ks[23], (DEPTH, D_MODEL), 0.02)
    ln2_b = nrm(ks[24], (DEPTH, D_MODEL), 0.02)
    return {'x': x, 'p': p, 'w_in': w_in,
            'ssm_lambda_re': ssm_lambda_re, 'ssm_lambda_im': ssm_lambda_im, 'ssm_log_dt': ssm_log_dt,
            'ssm_b_re': ssm_b_re, 'ssm_b_im': ssm_b_im, 'ssm_c_re': ssm_c_re, 'ssm_c_im': ssm_c_im,
            'ssm_d': ssm_d, 'ssm_glu_w': ssm_glu_w, 'ssm_glu_b': ssm_glu_b,
            'w_branch_a': w_branch_a, 'w_branch_b': w_branch_b, 'w_out': w_out,
            'ln1_g': ln1_g, 'ln1_b': ln1_b,
            'ffn_w_gate': ffn_w_gate, 'ffn_w_up': ffn_w_up, 'ffn_w_down': ffn_w_down,
            'ple_w_gate': ple_w_gate, 'ple_w_proj': ple_w_proj,
            'ln2_g': ln2_g, 'ln2_b': ln2_b}


def reference(x, p, w_in, ssm_lambda_re, ssm_lambda_im, ssm_log_dt, ssm_b_re, ssm_b_im,
              ssm_c_re, ssm_c_im, ssm_d, ssm_glu_w, ssm_glu_b, w_branch_a, w_branch_b, w_out,
              ln1_g, ln1_b, ffn_w_gate, ffn_w_up, ffn_w_down, ple_w_gate, ple_w_proj, ln2_g, ln2_b):
    splits = [SB_WIDTH, 2 * SB_WIDTH, 3 * SB_WIDTH, 3 * SB_WIDTH + SSM_WIDTH, 3 * SB_WIDTH + SSM_WIDTH + D_MODEL]
    h = x
    for i in range(DEPTH):
        proj = h @ w_in[i]
        q, k, v, u_ssm, g_a, g_b = jnp.split(proj, splits, axis=-1)
        o_a = stick_breaking_attention(q, k, v)
        o_b = s5_branch(u_ssm, ssm_lambda_re[i], ssm_lambda_im[i], ssm_log_dt[i],
                        ssm_b_re[i], ssm_b_im[i], ssm_c_re[i], ssm_c_im[i],
                        ssm_d[i], ssm_glu_w[i], ssm_glu_b[i])
        merged = (jax.nn.sigmoid(g_a) * (o_a @ w_branch_a[i])
                  + jax.nn.sigmoid(g_b) * (o_b @ w_branch_b[i]))
        h = layer_norm(DEEPNORM_ALPHA * h + merged @ w_out[i], ln1_g[i], ln1_b[i])
        ffn = (jax.nn.silu(h @ ffn_w_gate[i]) * (h @ ffn_w_up[i])) @ ffn_w_down[i]
        ple = jax.nn.sigmoid(h @ ple_w_gate[i]) * (p[i] @ ple_w_proj[i])
        h = layer_norm(DEEPNORM_ALPHA * h + ffn + ple, ln2_g[i], ln2_b[i])
    return h
```

```python
import functools

import jax
import jax.numpy as jnp
from jax import lax
from jax.experimental import pallas as pl
from jax.experimental.pallas import tpu as pltpu

F32 = jnp.float32
BF16 = jnp.bfloat16

LN_EPS = 1e-5
SB_HEADS = 8
SB_HEAD_DIM = 64
SB_WIDTH = SB_HEADS * SB_HEAD_DIM
SSM_GROUP = 16
SSM_STATE = 64

LANES = 128
SUBLANES = 8
HEADS_PER_BLOCK = LANES // SB_HEAD_DIM
VMEM_LIMIT = 56 * 1024 * 1024

ROW_TILE = 512
ATTN_TQ = 256
ATTN_TK = 128
S5_TT = 512


def _const_spec(shape):
    nd = len(shape)
    return pl.BlockSpec(shape, lambda *_: (0,) * nd, pipeline_mode=pl.Buffered(1))


def _layer_norm(r, g, b):
    mu = jnp.mean(r, axis=-1, keepdims=True)
    d = r - mu
    var = jnp.mean(d * d, axis=-1, keepdims=True)
    return d * lax.rsqrt(var + LN_EPS) * g + b


def _s5_discretize_kernel(lre_ref, lim_ref, ldt_ref, bre_ref, bim_ref,
                          lbre_ref, lbim_ref, bbre_ref, bbim_ref):
    lam_re = lre_ref[...]
    lam_im = lim_ref[...]
    dt = jnp.exp(ldt_ref[...])
    mag = jnp.exp(lam_re * dt)
    lb_re = mag * jnp.cos(lam_im * dt)
    lb_im = mag * jnp.sin(lam_im * dt)
    den = lam_re * lam_re + lam_im * lam_im
    n_re = lb_re - 1.0
    coef_re = (n_re * lam_re + lb_im * lam_im) / den
    coef_im = (lb_im * lam_re - n_re * lam_im) / den
    b_re = bre_ref[...]
    b_im = bim_ref[...]
    lbre_ref[...] = lb_re
    lbim_ref[...] = lb_im
    bbre_ref[...] = coef_re * b_re - coef_im * b_im
    bbim_ref[...] = coef_re * b_im + coef_im * b_re


def _s5_discretize(lam_re, lam_im, log_dt, b_re, b_im):
    g, p = lam_re.shape
    c = b_re.shape[-1]
    rep = lambda a: jnp.repeat(a, c, axis=1)
    ldt = jnp.broadcast_to(log_dt[:, None], (g, p * c))
    shp = jax.ShapeDtypeStruct((g, p * c), F32)
    lb_re, lb_im, bb_re, bb_im = pl.pallas_call(
        _s5_discretize_kernel, out_shape=(shp, shp, shp, shp), name="s5_discretize",
    )(rep(lam_re), rep(lam_im), ldt, b_re.reshape(g, p * c), b_im.reshape(g, p * c))
    return (lb_re[:, ::c], lb_im[:, ::c],
            bb_re.reshape(g, p, c), bb_im.reshape(g, p, c))


def _in_proj_kernel(x_ref, w_ref, q_ref, k_ref, v_ref, u_ref, ga_ref, gb_ref, *, q_scale):
    xb = x_ref[...].astype(BF16)

    def mm(c0, n):
        return jnp.dot(xb, w_ref[:, c0:c0 + n], preferred_element_type=F32)

    w = SB_WIDTH
    d = x_ref.shape[1]
    q_ref[...] = (mm(0, w) * q_scale).astype(BF16)
    k_ref[...] = mm(w, w).astype(BF16)
    v_ref[...] = mm(2 * w, w).astype(BF16)
    ssm_w = u_ref.shape[1]
    u_ref[...] = mm(3 * w, ssm_w)
    g0 = 3 * w + ssm_w
    ga_ref[...] = jax.nn.sigmoid(mm(g0, d)).astype(BF16)
    gb_ref[...] = jax.nn.sigmoid(mm(g0 + d, d)).astype(BF16)


def _in_proj(x2, w_in_bf, ssm_w):
    t, d = x2.shape
    tm = ROW_TILE
    row = lambda n: pl.BlockSpec((tm, n), lambda i: (i, 0))
    sds = lambda n, dt: jax.ShapeDtypeStruct((t, n), dt)
    return pl.pallas_call(
        functools.partial(_in_proj_kernel, q_scale=SB_HEAD_DIM ** -0.5),
        grid=(t // tm,),
        in_specs=[row(d), _const_spec(w_in_bf.shape)],
        out_specs=[row(SB_WIDTH), row(SB_WIDTH), row(SB_WIDTH), row(ssm_w), row(d), row(d)],
        out_shape=[sds(SB_WIDTH, BF16), sds(SB_WIDTH, BF16), sds(SB_WIDTH, BF16),
                   sds(ssm_w, F32), sds(d, BF16), sds(d, BF16)],
        compiler_params=pltpu.CompilerParams(dimension_semantics=("parallel",),
                                             vmem_limit_bytes=VMEM_LIMIT),
        name="in_proj",
    )(x2, w_in_bf)


def _suffix_sum_matrix():
    j = lax.broadcasted_iota(jnp.int32, (2 * LANES, 2 * LANES), 0) % LANES
    s = lax.broadcasted_iota(jnp.int32, (2 * LANES, 2 * LANES), 1)
    return jnp.where((s >= LANES) | (j > s), 1.0, 0.0).astype(BF16)


def _attn_kernel(q_ref, k_ref, v_ref, m2_ref, o_ref):
    tq = q_ref.shape[0]
    tk = ATTN_TK
    n_diag = tq // tk
    i = pl.program_id(2)
    lane = lax.broadcasted_iota(jnp.int32, (1, LANES), 1)
    row = lax.broadcasted_iota(jnp.int32, (tq, tk), 0)
    col = lax.broadcasted_iota(jnp.int32, (tq, tk), 1)
    q2 = q_ref[...]
    m2 = m2_ref[...]

    def tile(qh, j, acc, ls, mask):
        start = pl.multiple_of(j * tk, tk)
        kt = k_ref[pl.ds(start, tk), :]
        vt = v_ref[pl.ds(start, tk), :]
        z = lax.dot_general(qh, kt, (((1,), (1,)), ((), ())), preferred_element_type=F32)
        sp = jnp.maximum(z, 0.0) + jnp.log(1.0 + jnp.exp(-jnp.abs(z)))
        spm = sp if mask is None else jnp.where(mask, sp, 0.0)
        hi = spm.astype(BF16)
        lo = (spm - hi.astype(F32)).astype(BF16)
        cs = jnp.dot(jnp.concatenate([hi, lo], axis=1), m2, preferred_element_type=F32)
        w = jnp.exp((z - sp) - cs[:, :tk] + ls)
        if mask is not None:
            w = jnp.where(mask, w, 0.0)
        acc = acc + jnp.dot(w.astype(BF16), vt, preferred_element_type=F32)
        return acc, ls - cs[:, tk:]

    outs = []
    for hh in range(HEADS_PER_BLOCK):
        in_head = (lane >= hh * SB_HEAD_DIM) & (lane < (hh + 1) * SB_HEAD_DIM)
        qh = jnp.where(in_head, q2, jnp.zeros_like(q2))
        acc = jnp.zeros((tq, LANES), F32)
        ls = jnp.zeros((tq, LANES), F32)
        for r in reversed(range(n_diag)):
            acc, ls = tile(qh, i * n_diag + r, acc, ls, (col + r * tk) < row)

        def body(n, carry, qh=qh):
            return tile(qh, i * n_diag - 1 - n, carry[0], carry[1], None)

        acc, ls = lax.fori_loop(0, i * n_diag, body, (acc, ls))
        outs.append(acc)
    o_ref[...] = jnp.where(lane < SB_HEAD_DIM, outs[0], outs[1]).astype(o_ref.dtype)


def _attention(q, k, v, bsz, seq):
    t = q.shape[0]
    tq = ATTN_TQ
    nq = seq // tq
    n_blk = SB_WIDTH // LANES
    qspec = pl.BlockSpec((tq, LANES), lambda b, h, i: (b * nq + i, h))
    kvspec = pl.BlockSpec((seq, LANES), lambda b, h, i: (b, h))
    return pl.pallas_call(
        _attn_kernel,
        grid=(bsz, n_blk, nq),
        in_specs=[qspec, kvspec, kvspec, _const_spec((2 * LANES, 2 * LANES))],
        out_specs=qspec,
        out_shape=jax.ShapeDtypeStruct((t, SB_WIDTH), BF16),
        compiler_params=pltpu.CompilerParams(
            dimension_semantics=("parallel", "parallel", "arbitrary"),
            vmem_limit_bytes=VMEM_LIMIT),
        name="stickbreak_attention",
    )(q, k, v, _suffix_sum_matrix())


def _s5_kernel(u_ref, bre_ref, bim_ref, cre_ref, cim_ref, are_ref, aim_ref, d_ref,
               gw_ref, gb_ref, o_ref, sre_buf, sim_buf, cre_st, cim_st):
    bsz, tt, width = u_ref.shape
    n_chunk = bre_ref.shape[1] // LANES
    n_slab = n_chunk // SUBLANES
    n_ublk = width // LANES
    ch_per_ublk = n_chunk // n_ublk

    @pl.when(pl.program_id(0) == 0)
    def _():
        cre_st[...] = jnp.zeros_like(cre_st)
        cim_st[...] = jnp.zeros_like(cim_st)

    def slab_rows(c):
        return pl.ds(c % SUBLANES, tt, stride=SUBLANES)

    for b in range(bsz):
        ub = u_ref[b].astype(BF16)
        for blk in range(n_ublk):
            u_blk = ub[:, blk * LANES:(blk + 1) * LANES]
            for half in range(ch_per_ublk // 2):
                c0 = blk * ch_per_ublk + 2 * half
                cols = slice(c0 * LANES, (c0 + 2) * LANES)
                rows = slice(blk * LANES, (blk + 1) * LANES)
                for w_ref, buf in ((bre_ref, sre_buf), (bim_ref, sim_buf)):
                    val = jnp.dot(u_blk, w_ref[rows, cols], preferred_element_type=F32)
                    for e in range(2):
                        c = c0 + e
                        buf[b * n_slab + c // SUBLANES, slab_rows(c), :] = val[:, e * LANES:(e + 1) * LANES]

    a_re = [are_ref[s] for s in range(n_slab)]
    a_im = [aim_ref[s] for s in range(n_slab)]
    n_chain = bsz * n_slab

    def step(t, carry):
        base = pl.multiple_of(t * SUBLANES, SUBLANES)
        new = []
        for idx in range(n_chain):
            s_re, s_im = carry[idx]
            ar, ai = a_re[idx % n_slab], a_im[idx % n_slab]
            n_re = ar * s_re - ai * s_im + sre_buf[idx, pl.ds(base, SUBLANES), :]
            n_im = ar * s_im + ai * s_re + sim_buf[idx, pl.ds(base, SUBLANES), :]
            sre_buf[idx, pl.ds(base, SUBLANES), :] = n_re
            sim_buf[idx, pl.ds(base, SUBLANES), :] = n_im
            new.append((n_re, n_im))
        return tuple(new)

    init = tuple((cre_st[idx], cim_st[idx]) for idx in range(n_chain))
    final = lax.fori_loop(0, tt, step, init, unroll=8)
    for idx in range(n_chain):
        cre_st[idx] = final[idx][0]
        cim_st[idx] = final[idx][1]

    for b in range(bsz):
        ys = []
        for blk in range(n_ublk):
            acc = jnp.zeros((tt, LANES), F32)
            for half in range(ch_per_ublk // 2):
                c0 = blk * ch_per_ublk + 2 * half
                rows = slice(c0 * LANES, (c0 + 2) * LANES)
                cols = slice(blk * LANES, (blk + 1) * LANES)

                def chunk_pair(buf):
                    parts = [buf[b * n_slab + (c0 + e) // SUBLANES, slab_rows(c0 + e), :]
                             for e in range(2)]
                    return jnp.concatenate(parts, axis=1).astype(BF16)

                acc = acc + jnp.dot(chunk_pair(sre_buf), cre_ref[rows, cols], preferred_element_type=F32)
                acc = acc - jnp.dot(chunk_pair(sim_buf), cim_ref[rows, cols], preferred_element_type=F32)
            ys.append(acc)
        y = jnp.concatenate(ys, axis=1) + d_ref[...] * u_ref[b]
        y = jax.nn.gelu(y)
        gate = jnp.dot(y.astype(BF16), gw_ref[...], preferred_element_type=F32) + gb_ref[...]
        o_ref[b] = (y * jax.nn.sigmoid(gate)).astype(o_ref.dtype)


def _block_diag(blocks):
    g, r, c = blocks.shape
    eye = jnp.eye(g, dtype=blocks.dtype)
    return (eye[:, None, :, None] * blocks[:, :, None, :]).reshape(g * r, g * c)


def _s5_branch(u3, lam_re, lam_im, log_dt, b_re, b_im, c_re, c_im, d_skip, glu_w, glu_b):
    bsz, seq, width = u3.shape
    lb_re, lb_im, bb_re, bb_im = _s5_discretize(lam_re, lam_im, log_dt, b_re, b_im)
    n_state = lb_re.size
    n_slab = n_state // (SUBLANES * LANES)
    bmat_re = _block_diag(bb_re.transpose(0, 2, 1)).astype(BF16)
    bmat_im = _block_diag(bb_im.transpose(0, 2, 1)).astype(BF16)
    cmat_re = _block_diag(c_re.transpose(0, 2, 1)).astype(BF16)
    cmat_im = _block_diag(c_im.transpose(0, 2, 1)).astype(BF16)
    a_re = lb_re.reshape(n_slab, SUBLANES, LANES)
    a_im = lb_im.reshape(n_slab, SUBLANES, LANES)
    tt = S5_TT
    tile = pl.BlockSpec((bsz, tt, width), lambda i: (0, i, 0))
    slab_buf = pltpu.VMEM((bsz * n_slab, tt * SUBLANES, LANES), F32)
    carry = pltpu.VMEM((bsz * n_slab, SUBLANES, LANES), F32)
    return pl.pallas_call(
        _s5_kernel,
        grid=(seq // tt,),
        in_specs=[tile, _const_spec(bmat_re.shape), _const_spec(bmat_im.shape),
                  _const_spec(cmat_re.shape), _const_spec(cmat_im.shape),
                  _const_spec(a_re.shape), _const_spec(a_im.shape),
                  _const_spec((1, width)), _const_spec(glu_w.shape), _const_spec((1, width))],
        out_specs=tile,
        out_shape=jax.ShapeDtypeStruct((bsz, seq, width), BF16),
        scratch_shapes=[slab_buf, slab_buf, carry, carry],
        compiler_params=pltpu.CompilerParams(dimension_semantics=("arbitrary",),
                                             vmem_limit_bytes=VMEM_LIMIT),
        name="s5_branch",
    )(u3, bmat_re, bmat_im, cmat_re, cmat_im, a_re, a_im,
      d_skip.reshape(1, width), glu_w.astype(BF16), glu_b.reshape(1, width))


def _merge_ln1_kernel(x_ref, oa_ref, ob_ref, ga_ref, gb_ref, wa_ref, wb_ref, wo_ref,
                      g_ref, b_ref, h_ref, *, alpha):
    pa = jnp.dot(oa_ref[...], wa_ref[...], preferred_element_type=F32)
    pb = jnp.dot(ob_ref[...], wb_ref[...], preferred_element_type=F32)
    merged = ga_ref[...].astype(F32) * pa + gb_ref[...].astype(F32) * pb
    r = alpha * x_ref[...] + jnp.dot(merged.astype(BF16), wo_ref[...], preferred_element_type=F32)
    h_ref[...] = _layer_norm(r, g_ref[...], b_ref[...])


def _merge_ln1(x2, o_a, o_b, sg_a, sg_b, w_a, w_b, w_out, ln_g, ln_b, alpha):
    t, d = x2.shape
    tm = ROW_TILE
    row = lambda n: pl.BlockSpec((tm, n), lambda i: (i, 0))
    return pl.pallas_call(
        functools.partial(_merge_ln1_kernel, alpha=alpha),
        grid=(t // tm,),
        in_specs=[row(d), row(o_a.shape[1]), row(o_b.shape[1]), row(d), row(d),
                  _const_spec(w_a.shape), _const_spec(w_b.shape), _const_spec(w_out.shape),
                  _const_spec((1, d)), _const_spec((1, d))],
        out_specs=row(d),
        out_shape=jax.ShapeDtypeStruct((t, d), F32),
        compiler_params=pltpu.CompilerParams(dimension_semantics=("parallel",),
                                             vmem_limit_bytes=VMEM_LIMIT),
        name="merge_ln1",
    )(x2, o_a, o_b, sg_a, sg_b, w_a, w_b, w_out, ln_g.reshape(1, d), ln_b.reshape(1, d))


def _ffn_ple_ln2_kernel(h_ref, p_ref, wg_ref, wu_ref, wd_ref, wpg_ref, wpp_ref,
                        g_ref, b_ref, o_ref, *, alpha, ff_chunk):
    h = h_ref[...]
    hb = h.astype(BF16)
    d_ff = wg_ref.shape[1]
    ffn = jnp.zeros(h.shape, F32)
    for c0 in range(0, d_ff, ff_chunk):
        gate = jnp.dot(hb, wg_ref[:, c0:c0 + ff_chunk], preferred_element_type=F32)
        up = jnp.dot(hb, wu_ref[:, c0:c0 + ff_chunk], preferred_element_type=F32)
        act = (jax.nn.silu(gate) * up).astype(BF16)
        ffn = ffn + jnp.dot(act, wd_ref[c0:c0 + ff_chunk, :], preferred_element_type=F32)
    ple_gate = jax.nn.sigmoid(jnp.dot(hb, wpg_ref[...], preferred_element_type=F32))
    ple = ple_gate * jnp.dot(p_ref[...].astype(BF16), wpp_ref[...], preferred_element_type=F32)
    o_ref[...] = _layer_norm(alpha * h + ffn + ple, g_ref[...], b_ref[...])


def _ffn_ple_ln2(h1, p2, w_gate, w_up, w_down, w_pg, w_pp, ln_g, ln_b, alpha):
    t, d = h1.shape
    tm = ROW_TILE
    d_ff = w_gate.shape[1]
    ff_chunk = d_ff // 2 if (d_ff // 2) % LANES == 0 else d_ff
    row = lambda n: pl.BlockSpec((tm, n), lambda i: (i, 0))
    return pl.pallas_call(
        functools.partial(_ffn_ple_ln2_kernel, alpha=alpha, ff_chunk=ff_chunk),
        grid=(t // tm,),
        in_specs=[row(d), row(p2.shape[1]),
                  _const_spec(w_gate.shape), _const_spec(w_up.shape), _const_spec(w_down.shape),
                  _const_spec(w_pg.shape), _const_spec(w_pp.shape),
                  _const_spec((1, d)), _const_spec((1, d))],
        out_specs=row(d),
        out_shape=jax.ShapeDtypeStruct((t, d), F32),
        compiler_params=pltpu.CompilerParams(dimension_semantics=("parallel",),
                                             vmem_limit_bytes=VMEM_LIMIT),
        name="ffn_ple_ln2",
    )(h1, p2, w_gate, w_up, w_down, w_pg, w_pp, ln_g.reshape(1, d), ln_b.reshape(1, d))


def kernel(x, p, w_in, ssm_lambda_re, ssm_lambda_im, ssm_log_dt, ssm_b_re, ssm_b_im, ssm_c_re, ssm_c_im, ssm_d, ssm_glu_w, ssm_glu_b, w_branch_a, w_branch_b, w_out, ln1_g, ln1_b, ffn_w_gate, ffn_w_up, ffn_w_down, ple_w_gate, ple_w_proj, ln2_g, ln2_b):
    bsz, seq, d = x.shape
    depth = w_in.shape[0]
    alpha = (2 * depth) ** 0.25
    ssm_w = ssm_d.shape[1]
    t = bsz * seq
    h = x.reshape(t, d)
    for i in range(depth):
        q, k, v, u, sg_a, sg_b = _in_proj(h, w_in[i].astype(BF16), ssm_w)
        o_a = _attention(q, k, v, bsz, seq)
        o_b = _s5_branch(u.reshape(bsz, seq, ssm_w), ssm_lambda_re[i], ssm_lambda_im[i], ssm_log_dt[i],
                         ssm_b_re[i], ssm_b_im[i], ssm_c_re[i], ssm_c_im[i],
                         ssm_d[i], ssm_glu_w[i], ssm_glu_b[i]).reshape(t, ssm_w)
        h = _merge_ln1(h, o_a, o_b, sg_a, sg_b, w_branch_a[i].astype(BF16), w_branch_b[i].astype(BF16),
                       w_out[i].astype(BF16), ln1_g[i], ln1_b[i], alpha)
        h = _ffn_ple_ln2(h, p[i].reshape(t, -1), ffn_w_gate[i].astype(BF16), ffn_w_up[i].astype(BF16),
                         ffn_w_down[i].astype(BF16), ple_w_gate[i].astype(BF16),
                         ple_w_proj[i].astype(BF16), ln2_g[i], ln2_b[i], alpha)
    return h.reshape(bsz, seq, d)
```

```python
import functools

import jax
import jax.numpy as jnp
from jax import lax
from jax.experimental import pallas as pl
from jax.experimental.pallas import tpu as pltpu

F32 = jnp.float32
BF16 = jnp.bfloat16

LN_EPS = 1e-5
SB_HEADS = 8
SB_HEAD_DIM = 64
SB_WIDTH = SB_HEADS * SB_HEAD_DIM
SSM_GROUP = 16
SSM_STATE = 64

LANES = 128
SUBLANES = 8
SUFFIX_UNIT = 2 * LANES
LOG2_E = 1.4426950408889634
SIGN_BIT = 0x80000000
VMEM_LIMIT = 56 * 1024 * 1024

ROW_TILE = 512
ATTN_TQ = 512
S5_TT = 512


def _const_spec(shape):
    nd = len(shape)
    return pl.BlockSpec(shape, lambda *_: (0,) * nd, pipeline_mode=pl.Buffered(1))


def _layer_norm(r, g, b):
    mu = jnp.mean(r, axis=-1, keepdims=True)
    d = r - mu
    var = jnp.mean(d * d, axis=-1, keepdims=True)
    return d * lax.rsqrt(var + LN_EPS) * g + b


def _s5_discretize_kernel(lre_ref, lim_ref, ldt_ref, bre_ref, bim_ref,
                          lbre_ref, lbim_ref, bbre_ref, bbim_ref):
    lam_re = lre_ref[...]
    lam_im = lim_ref[...]
    dt = jnp.exp(ldt_ref[...])
    mag = jnp.exp(lam_re * dt)
    lb_re = mag * jnp.cos(lam_im * dt)
    lb_im = mag * jnp.sin(lam_im * dt)
    den = lam_re * lam_re + lam_im * lam_im
    n_re = lb_re - 1.0
    coef_re = (n_re * lam_re + lb_im * lam_im) / den
    coef_im = (lb_im * lam_re - n_re * lam_im) / den
    b_re = bre_ref[...]
    b_im = bim_ref[...]
    lbre_ref[...] = lb_re
    lbim_ref[...] = lb_im
    bbre_ref[...] = coef_re * b_re - coef_im * b_im
    bbim_ref[...] = coef_re * b_im + coef_im * b_re


def _s5_discretize(lam_re, lam_im, log_dt, b_re, b_im):
    g, p = lam_re.shape
    c = b_re.shape[-1]
    rep = lambda a: jnp.repeat(a, c, axis=1)
    ldt = jnp.broadcast_to(log_dt[:, None], (g, p * c))
    shp = jax.ShapeDtypeStruct((g, p * c), F32)
    lb_re, lb_im, bb_re, bb_im = pl.pallas_call(
        _s5_discretize_kernel, out_shape=(shp, shp, shp, shp), name="s5_discretize",
    )(rep(lam_re), rep(lam_im), ldt, b_re.reshape(g, p * c), b_im.reshape(g, p * c))
    return (lb_re[:, ::c], lb_im[:, ::c],
            bb_re.reshape(g, p, c), bb_im.reshape(g, p, c))


def _in_proj_kernel(x_ref, w_ref, q_ref, k_ref, v_ref, u_ref, ga_ref, gb_ref, *, q_scale):
    xb = x_ref[...].astype(BF16)

    def mm(c0, n):
        return jnp.dot(xb, w_ref[:, c0:c0 + n], preferred_element_type=F32)

    w = SB_WIDTH
    d = x_ref.shape[1]
    q_ref[...] = (mm(0, w) * q_scale).astype(BF16)
    k_ref[...] = mm(w, w).astype(BF16)
    v_ref[...] = mm(2 * w, w).astype(BF16)
    ssm_w = u_ref.shape[1]
    u_ref[...] = mm(3 * w, ssm_w)
    g0 = 3 * w + ssm_w
    ga_ref[...] = jax.nn.sigmoid(mm(g0, d)).astype(BF16)
    gb_ref[...] = jax.nn.sigmoid(mm(g0 + d, d)).astype(BF16)


def _in_proj(x2, w_in_bf, ssm_w):
    t, d = x2.shape
    tm = ROW_TILE
    row = lambda n: pl.BlockSpec((tm, n), lambda i: (i, 0))
    sds = lambda n, dt: jax.ShapeDtypeStruct((t, n), dt)
    return pl.pallas_call(
        functools.partial(_in_proj_kernel, q_scale=LOG2_E * SB_HEAD_DIM ** -0.5),
        grid=(t // tm,),
        in_specs=[row(d), _const_spec(w_in_bf.shape)],
        out_specs=[row(SB_WIDTH), row(SB_WIDTH), row(SB_WIDTH), row(ssm_w), row(d), row(d)],
        out_shape=[sds(SB_WIDTH, BF16), sds(SB_WIDTH, BF16), sds(SB_WIDTH, BF16),
                   sds(ssm_w, F32), sds(d, BF16), sds(d, BF16)],
        compiler_params=pltpu.CompilerParams(dimension_semantics=("parallel",),
                                             vmem_limit_bytes=VMEM_LIMIT),
        name="in_proj",
    )(x2, w_in_bf)


def _suffix_sum_matrix():
    j = lax.broadcasted_iota(jnp.int32, (SUFFIX_UNIT, SUFFIX_UNIT), 0)
    s = lax.broadcasted_iota(jnp.int32, (SUFFIX_UNIT, SUFFIX_UNIT), 1)
    return jnp.where(j > s, 1.0, 0.0).astype(BF16)


def _attn_kernel(q_ref, k_ref, v_ref, m_ref, o_ref):
    tq = q_ref.shape[0]
    n_unit = tq // SUFFIX_UNIT
    i = pl.program_id(2)
    lane = lax.broadcasted_iota(jnp.int32, (1, LANES), 1)
    q2 = q_ref[...]
    zero = jnp.zeros_like(q2)
    qs = jnp.concatenate([jnp.where(lane < SB_HEAD_DIM, q2, zero),
                          jnp.where(lane >= SB_HEAD_DIM, q2, zero)], axis=0)
    m = m_ref[...]

    def strip(j, acc, ls, mask):
        start = pl.multiple_of(j * tq, tq)
        ks = k_ref[pl.ds(start, tq), :]
        vs = v_ref[pl.ds(start, tq), :]
        z = lax.dot_general(qs, ks, (((1,), (1,)), ((), ())), preferred_element_type=F32)
        neg_abs = pltpu.bitcast(pltpu.bitcast(z, jnp.uint32) | jnp.uint32(SIGN_BIT), F32)
        sp = jnp.maximum(z, 0.0) + jnp.log(1.0 + jnp.exp2(neg_abs)) * LOG2_E
        d = z - sp
        hi = (sp if mask is None else jnp.where(mask, sp, 0.0)).astype(BF16)
        ws = [None] * n_unit
        for u in reversed(range(n_unit)):
            sl = slice(u * SUFFIX_UNIT, (u + 1) * SUFFIX_UNIT)
            suffix = jnp.dot(hi[:, sl], m, preferred_element_type=F32)
            ws[u] = jnp.exp2(d[:, sl] - suffix + jnp.concatenate([ls, ls], axis=1))
            first = u * SUFFIX_UNIT
            total = suffix[:, :1] + hi[:, first:first + 1].astype(F32)
            ls = ls - total
        w = jnp.concatenate(ws, axis=1)
        if mask is not None:
            w = jnp.where(mask, w, 0.0)
        acc = acc + jnp.dot(w.astype(BF16), vs, preferred_element_type=F32)
        return acc, ls

    row = lax.broadcasted_iota(jnp.int32, (2 * tq, tq), 0)
    col = lax.broadcasted_iota(jnp.int32, (2 * tq, tq), 1)
    causal = col < jnp.where(row >= tq, row - tq, row)
    acc = jnp.zeros((2 * tq, LANES), F32)
    ls = jnp.zeros((2 * tq, LANES), F32)
    acc, ls = strip(i, acc, ls, causal)
    acc, ls = lax.fori_loop(0, i, lambda n, c: strip(i - 1 - n, c[0], c[1], None), (acc, ls))
    o_ref[...] = jnp.where(lane < SB_HEAD_DIM, acc[:tq], acc[tq:]).astype(o_ref.dtype)


def _attention(q, k, v, bsz, seq):
    t = q.shape[0]
    tq = ATTN_TQ
    nq = seq // tq
    n_blk = SB_WIDTH // LANES
    qspec = pl.BlockSpec((tq, LANES), lambda b, h, i: (b * nq + i, h))
    kvspec = pl.BlockSpec((seq, LANES), lambda b, h, i: (b, h))
    return pl.pallas_call(
        _attn_kernel,
        grid=(bsz, n_blk, nq),
        in_specs=[qspec, kvspec, kvspec, _const_spec((SUFFIX_UNIT, SUFFIX_UNIT))],
        out_specs=qspec,
        out_shape=jax.ShapeDtypeStruct((t, SB_WIDTH), BF16),
        compiler_params=pltpu.CompilerParams(
            dimension_semantics=("parallel", "parallel", "arbitrary"),
            vmem_limit_bytes=VMEM_LIMIT),
        name="stickbreak_attention",
    )(q, k, v, _suffix_sum_matrix())


def _s5_kernel(u_ref, bre_ref, bim_ref, cre_ref, cim_ref, are_ref, aim_ref, d_ref,
               gw_ref, gb_ref, o_ref, sre_buf, sim_buf, cre_st, cim_st):
    bsz, tt, width = u_ref.shape
    n_chunk = bre_ref.shape[1] // LANES
    n_slab = n_chunk // SUBLANES
    n_ublk = width // LANES
    ch_per_ublk = n_chunk // n_ublk

    @pl.when(pl.program_id(0) == 0)
    def _():
        cre_st[...] = jnp.zeros_like(cre_st)
        cim_st[...] = jnp.zeros_like(cim_st)

    def slab_rows(c):
        return pl.ds(c % SUBLANES, tt, stride=SUBLANES)

    for b in range(bsz):
        ub = u_ref[b].astype(BF16)
        for blk in range(n_ublk):
            u_blk = ub[:, blk * LANES:(blk + 1) * LANES]
            for half in range(ch_per_ublk // 2):
                c0 = blk * ch_per_ublk + 2 * half
                cols = slice(c0 * LANES, (c0 + 2) * LANES)
                rows = slice(blk * LANES, (blk + 1) * LANES)
                for w_ref, buf in ((bre_ref, sre_buf), (bim_ref, sim_buf)):
                    val = jnp.dot(u_blk, w_ref[rows, cols], preferred_element_type=F32)
                    for e in range(2):
                        c = c0 + e
                        buf[b * n_slab + c // SUBLANES, slab_rows(c), :] = val[:, e * LANES:(e + 1) * LANES]

    a_re = [are_ref[s] for s in range(n_slab)]
    a_im = [aim_ref[s] for s in range(n_slab)]
    n_chain = bsz * n_slab

    def step(t, carry):
        base = pl.multiple_of(t * SUBLANES, SUBLANES)
        new = []
        for idx in range(n_chain):
            s_re, s_im = carry[idx]
            ar, ai = a_re[idx % n_slab], a_im[idx % n_slab]
            n_re = ar * s_re - ai * s_im + sre_buf[idx, pl.ds(base, SUBLANES), :]
            n_im = ar * s_im + ai * s_re + sim_buf[idx, pl.ds(base, SUBLANES), :]
            sre_buf[idx, pl.ds(base, SUBLANES), :] = n_re
            sim_buf[idx, pl.ds(base, SUBLANES), :] = n_im
            new.append((n_re, n_im))
        return tuple(new)

    init = tuple((cre_st[idx], cim_st[idx]) for idx in range(n_chain))
    final = lax.fori_loop(0, tt, step, init, unroll=8)
    for idx in range(n_chain):
        cre_st[idx] = final[idx][0]
        cim_st[idx] = final[idx][1]

    for b in range(bsz):
        ys = []
        for blk in range(n_ublk):
            acc = jnp.zeros((tt, LANES), F32)
            for half in range(ch_per_ublk // 2):
                c0 = blk * ch_per_ublk + 2 * half
                rows = slice(c0 * LANES, (c0 + 2) * LANES)
                cols = slice(blk * LANES, (blk + 1) * LANES)

                def chunk_pair(buf):
                    parts = [buf[b * n_slab + (c0 + e) // SUBLANES, slab_rows(c0 + e), :]
                             for e in range(2)]
                    return jnp.concatenate(parts, axis=1).astype(BF16)

                acc = acc + jnp.dot(chunk_pair(sre_buf), cre_ref[rows, cols], preferred_element_type=F32)
                acc = acc - jnp.dot(chunk_pair(sim_buf), cim_ref[rows, cols], preferred_element_type=F32)
            ys.append(acc)
        y = jnp.concatenate(ys, axis=1) + d_ref[...] * u_ref[b]
        y = jax.nn.gelu(y)
        gate = jnp.dot(y.astype(BF16), gw_ref[...], preferred_element_type=F32) + gb_ref[...]
        o_ref[b] = (y * jax.nn.sigmoid(gate)).astype(o_ref.dtype)


def _block_diag(blocks):
    g, r, c = blocks.shape
    eye = jnp.eye(g, dtype=blocks.dtype)
    return (eye[:, None, :, None] * blocks[:, :, None, :]).reshape(g * r, g * c)


def _s5_branch(u3, lam_re, lam_im, log_dt, b_re, b_im, c_re, c_im, d_skip, glu_w, glu_b):
    bsz, seq, width = u3.shape
    lb_re, lb_im, bb_re, bb_im = _s5_discretize(lam_re, lam_im, log_dt, b_re, b_im)
    n_state = lb_re.size
    n_slab = n_state // (SUBLANES * LANES)
    bmat_re = _block_diag(bb_re.transpose(0, 2, 1)).astype(BF16)
    bmat_im = _block_diag(bb_im.transpose(0, 2, 1)).astype(BF16)
    cmat_re = _block_diag(c_re.transpose(0, 2, 1)).astype(BF16)
    cmat_im = _block_diag(c_im.transpose(0, 2, 1)).astype(BF16)
    a_re = lb_re.reshape(n_slab, SUBLANES, LANES)
    a_im = lb_im.reshape(n_slab, SUBLANES, LANES)
    tt = S5_TT
    tile = pl.BlockSpec((bsz, tt, width), lambda i: (0, i, 0))
    slab_buf = pltpu.VMEM((bsz * n_slab, tt * SUBLANES, LANES), F32)
    carry = pltpu.VMEM((bsz * n_slab, SUBLANES, LANES), F32)
    return pl.pallas_call(
        _s5_kernel,
        grid=(seq // tt,),
        in_specs=[tile, _const_spec(bmat_re.shape), _const_spec(bmat_im.shape),
                  _const_spec(cmat_re.shape), _const_spec(cmat_im.shape),
                  _const_spec(a_re.shape), _const_spec(a_im.shape),
                  _const_spec((1, width)), _const_spec(glu_w.shape), _const_spec((1, width))],
        out_specs=tile,
        out_shape=jax.ShapeDtypeStruct((bsz, seq, width), BF16),
        scratch_shapes=[slab_buf, slab_buf, carry, carry],
        compiler_params=pltpu.CompilerParams(dimension_semantics=("arbitrary",),
                                             vmem_limit_bytes=VMEM_LIMIT),
        name="s5_branch",
    )(u3, bmat_re, bmat_im, cmat_re, cmat_im, a_re, a_im,
      d_skip.reshape(1, width), glu_w.astype(BF16), glu_b.reshape(1, width))


def _merge_ln1_kernel(x_ref, oa_ref, ob_ref, ga_ref, gb_ref, wa_ref, wb_ref, wo_ref,
                      g_ref, b_ref, h_ref, *, alpha):
    pa = jnp.dot(oa_ref[...], wa_ref[...], preferred_element_type=F32)
    pb = jnp.dot(ob_ref[...], wb_ref[...], preferred_element_type=F32)
    merged = ga_ref[...].astype(F32) * pa + gb_ref[...].astype(F32) * pb
    r = alpha * x_ref[...] + jnp.dot(merged.astype(BF16), wo_ref[...], preferred_element_type=F32)
    h_ref[...] = _layer_norm(r, g_ref[...], b_ref[...])


def _merge_ln1(x2, o_a, o_b, sg_a, sg_b, w_a, w_b, w_out, ln_g, ln_b, alpha):
    t, d = x2.shape
    tm = ROW_TILE
    row = lambda n: pl.BlockSpec((tm, n), lambda i: (i, 0))
    return pl.pallas_call(
        functools.partial(_merge_ln1_kernel, alpha=alpha),
        grid=(t // tm,),
        in_specs=[row(d), row(o_a.shape[1]), row(o_b.shape[1]), row(d), row(d),
                  _const_spec(w_a.shape), _const_spec(w_b.shape), _const_spec(w_out.shape),
                  _const_spec((1, d)), _const_spec((1, d))],
        out_specs=row(d),
        out_shape=jax.ShapeDtypeStruct((t, d), F32),
        compiler_params=pltpu.CompilerParams(dimension_semantics=("parallel",),
                                             vmem_limit_bytes=VMEM_LIMIT),
        name="merge_ln1",
    )(x2, o_a, o_b, sg_a, sg_b, w_a, w_b, w_out, ln_g.reshape(1, d), ln_b.reshape(1, d))


def _ffn_ple_ln2_kernel(h_ref, p_ref, wg_ref, wu_ref, wd_ref, wpg_ref, wpp_ref,
                        g_ref, b_ref, o_ref, *, alpha, ff_chunk):
    h = h_ref[...]
    hb = h.astype(BF16)
    d_ff = wg_ref.shape[1]
    ffn = jnp.zeros(h.shape, F32)
    for c0 in range(0, d_ff, ff_chunk):
        gate = jnp.dot(hb, wg_ref[:, c0:c0 + ff_chunk], preferred_element_type=F32)
        up = jnp.dot(hb, wu_ref[:, c0:c0 + ff_chunk], preferred_element_type=F32)
        act = (jax.nn.silu(gate) * up).astype(BF16)
        ffn = ffn + jnp.dot(act, wd_ref[c0:c0 + ff_chunk, :], preferred_element_type=F32)
    ple_gate = jax.nn.sigmoid(jnp.dot(hb, wpg_ref[...], preferred_element_type=F32))
    ple = ple_gate * jnp.dot(p_ref[...].astype(BF16), wpp_ref[...], preferred_element_type=F32)
    o_ref[...] = _layer_norm(alpha * h + ffn + ple, g_ref[...], b_ref[...])


def _ffn_ple_ln2(h1, p2, w_gate, w_up, w_down, w_pg, w_pp, ln_g, ln_b, alpha):
    t, d = h1.shape
    tm = ROW_TILE
    d_ff = w_gate.shape[1]
    ff_chunk = d_ff // 2 if (d_ff // 2) % LANES == 0 else d_ff
    row = lambda n: pl.BlockSpec((tm, n), lambda i: (i, 0))
    return pl.pallas_call(
        functools.partial(_ffn_ple_ln2_kernel, alpha=alpha, ff_chunk=ff_chunk),
        grid=(t // tm,),
        in_specs=[row(d), row(p2.shape[1]),
                  _const_spec(w_gate.shape), _const_spec(w_up.shape), _const_spec(w_down.shape),
                  _const_spec(w_pg.shape), _const_spec(w_pp.shape),
                  _const_spec((1, d)), _const_spec((1, d))],
        out_specs=row(d),
        out_shape=jax.ShapeDtypeStruct((t, d), F32),
        compiler_params=pltpu.CompilerParams(dimension_semantics=("parallel",),
                                             vmem_limit_bytes=VMEM_LIMIT),
        name="ffn_ple_ln2",
    )(h1, p2, w_gate, w_up, w_down, w_pg, w_pp, ln_g.reshape(1, d), ln_b.reshape(1, d))


def kernel(x, p, w_in, ssm_lambda_re, ssm_lambda_im, ssm_log_dt, ssm_b_re, ssm_b_im, ssm_c_re, ssm_c_im, ssm_d, ssm_glu_w, ssm_glu_b, w_branch_a, w_branch_b, w_out, ln1_g, ln1_b, ffn_w_gate, ffn_w_up, ffn_w_down, ple_w_gate, ple_w_proj, ln2_g, ln2_b):
    bsz, seq, d = x.shape
    depth = w_in.shape[0]
    alpha = (2 * depth) ** 0.25
    ssm_w = ssm_d.shape[1]
    t = bsz * seq
    h = x.reshape(t, d)
    for i in range(depth):
        q, k, v, u, sg_a, sg_b = _in_proj(h, w_in[i].astype(BF16), ssm_w)
        o_a = _attention(q, k, v, bsz, seq)
        o_b = _s5_branch(u.reshape(bsz, seq, ssm_w), ssm_lambda_re[i], ssm_lambda_im[i], ssm_log_dt[i],
                         ssm_b_re[i], ssm_b_im[i], ssm_c_re[i], ssm_c_im[i],
                         ssm_d[i], ssm_glu_w[i], ssm_glu_b[i]).reshape(t, ssm_w)
        h = _merge_ln1(h, o_a, o_b, sg_a, sg_b, w_branch_a[i].astype(BF16), w_branch_b[i].astype(BF16),
                       w_out[i].astype(BF16), ln1_g[i], ln1_b[i], alpha)
        h = _ffn_ple_ln2(h, p[i].reshape(t, -1), ffn_w_gate[i].astype(BF16), ffn_w_up[i].astype(BF16),
                         ffn_w_down[i].astype(BF16), ple_w_gate[i].astype(BF16),
                         ple_w_proj[i].astype(BF16), ln2_g[i], ln2_b[i], alpha)
    return h.reshape(bsz, seq, d)
```

```python
import functools

import jax
import jax.numpy as jnp
from jax import lax
from jax.experimental import pallas as pl
from jax.experimental.pallas import tpu as pltpu

F32 = jnp.float32
BF16 = jnp.bfloat16

LN_EPS = 1e-5
SB_HEADS = 8
SB_HEAD_DIM = 64
SB_WIDTH = SB_HEADS * SB_HEAD_DIM
SSM_GROUP = 16
SSM_STATE = 64

LANES = 128
SUBLANES = 8
SUFFIX_UNIT = 2 * LANES
LOG2_E = 1.4426950408889634
SP_LINEAR_FROM = 126.0
LS_FLOOR = -200.0
VMEM_LIMIT = 56 * 1024 * 1024

ROW_TILE = 512
ATTN_TQ = 512
S5_TT = 512


def _const_spec(shape):
    nd = len(shape)
    return pl.BlockSpec(shape, lambda *_: (0,) * nd, pipeline_mode=pl.Buffered(1))


def _layer_norm(r, g, b):
    mu = jnp.mean(r, axis=-1, keepdims=True)
    d = r - mu
    var = jnp.mean(d * d, axis=-1, keepdims=True)
    return d * lax.rsqrt(var + LN_EPS) * g + b


def _s5_discretize_kernel(lre_ref, lim_ref, ldt_ref, bre_ref, bim_ref,
                          lbre_ref, lbim_ref, bbre_ref, bbim_ref):
    lam_re = lre_ref[...]
    lam_im = lim_ref[...]
    dt = jnp.exp(ldt_ref[...])
    mag = jnp.exp(lam_re * dt)
    lb_re = mag * jnp.cos(lam_im * dt)
    lb_im = mag * jnp.sin(lam_im * dt)
    den = lam_re * lam_re + lam_im * lam_im
    n_re = lb_re - 1.0
    coef_re = (n_re * lam_re + lb_im * lam_im) / den
    coef_im = (lb_im * lam_re - n_re * lam_im) / den
    b_re = bre_ref[...]
    b_im = bim_ref[...]
    lbre_ref[...] = lb_re
    lbim_ref[...] = lb_im
    bbre_ref[...] = coef_re * b_re - coef_im * b_im
    bbim_ref[...] = coef_re * b_im + coef_im * b_re


def _s5_discretize(lam_re, lam_im, log_dt, b_re, b_im):
    g, p = lam_re.shape
    c = b_re.shape[-1]
    rep = lambda a: jnp.repeat(a, c, axis=1)
    ldt = jnp.broadcast_to(log_dt[:, None], (g, p * c))
    shp = jax.ShapeDtypeStruct((g, p * c), F32)
    lb_re, lb_im, bb_re, bb_im = pl.pallas_call(
        _s5_discretize_kernel, out_shape=(shp, shp, shp, shp), name="s5_discretize",
    )(rep(lam_re), rep(lam_im), ldt, b_re.reshape(g, p * c), b_im.reshape(g, p * c))
    return (lb_re[:, ::c], lb_im[:, ::c],
            bb_re.reshape(g, p, c), bb_im.reshape(g, p, c))


def _in_proj_kernel(x_ref, w_ref, q_ref, k_ref, v_ref, u_ref, ga_ref, gb_ref, *, q_scale):
    xb = x_ref[...].astype(BF16)

    def mm(c0, n):
        return jnp.dot(xb, w_ref[:, c0:c0 + n], preferred_element_type=F32)

    w = SB_WIDTH
    d = x_ref.shape[1]
    q_ref[...] = (mm(0, w) * q_scale).astype(BF16)
    k_ref[...] = mm(w, w).astype(BF16)
    v_ref[...] = mm(2 * w, w).astype(BF16)
    ssm_w = u_ref.shape[1]
    u_ref[...] = mm(3 * w, ssm_w)
    g0 = 3 * w + ssm_w
    ga_ref[...] = jax.nn.sigmoid(mm(g0, d)).astype(BF16)
    gb_ref[...] = jax.nn.sigmoid(mm(g0 + d, d)).astype(BF16)


def _in_proj(x2, w_in_bf, ssm_w):
    t, d = x2.shape
    tm = ROW_TILE
    row = lambda n: pl.BlockSpec((tm, n), lambda i: (i, 0))
    sds = lambda n, dt: jax.ShapeDtypeStruct((t, n), dt)
    return pl.pallas_call(
        functools.partial(_in_proj_kernel, q_scale=LOG2_E * SB_HEAD_DIM ** -0.5),
        grid=(t // tm,),
        in_specs=[row(d), _const_spec(w_in_bf.shape)],
        out_specs=[row(SB_WIDTH), row(SB_WIDTH), row(SB_WIDTH), row(ssm_w), row(d), row(d)],
        out_shape=[sds(SB_WIDTH, BF16), sds(SB_WIDTH, BF16), sds(SB_WIDTH, BF16),
                   sds(ssm_w, F32), sds(d, BF16), sds(d, BF16)],
        compiler_params=pltpu.CompilerParams(dimension_semantics=("parallel",),
                                             vmem_limit_bytes=VMEM_LIMIT),
        name="in_proj",
    )(x2, w_in_bf)


def _suffix_sum_matrix():
    j = lax.broadcasted_iota(jnp.int32, (SUFFIX_UNIT, SUFFIX_UNIT), 0)
    s = lax.broadcasted_iota(jnp.int32, (SUFFIX_UNIT, SUFFIX_UNIT), 1)
    return jnp.where(j > s, 1.0, 0.0).astype(BF16)


def _attn_kernel(q_ref, k_ref, v_ref, m_ref, o_ref):
    tq = q_ref.shape[0]
    i = pl.program_id(2)
    lane = lax.broadcasted_iota(jnp.int32, (1, LANES), 1)
    q2 = q_ref[...]
    zero = jnp.zeros_like(q2)
    qs = jnp.concatenate([jnp.where(lane < SB_HEAD_DIM, q2, zero),
                          jnp.where(lane >= SB_HEAD_DIM, q2, zero)], axis=0)
    m = m_ref[...]

    def strip(start, width, acc, ls, mask):
        ks = k_ref[pl.ds(start, width), :]
        vs = v_ref[pl.ds(start, width), :]
        z = lax.dot_general(qs, ks, (((1,), (1,)), ((), ())), preferred_element_type=F32)
        sp = jnp.maximum(z, jnp.log(1.0 + jnp.exp2(jnp.minimum(z, SP_LINEAR_FROM))) * LOG2_E)
        d = z - sp
        hi = (sp if mask is None else jnp.where(mask, sp, 0.0)).astype(BF16)
        n_unit = width // SUFFIX_UNIT
        ws = [None] * n_unit
        for u in reversed(range(n_unit)):
            sl = slice(u * SUFFIX_UNIT, (u + 1) * SUFFIX_UNIT)
            suffix = jnp.dot(hi[:, sl], m, preferred_element_type=F32)
            ws[u] = jnp.exp2(d[:, sl] - suffix + jnp.concatenate([ls, ls], axis=1))
            first = u * SUFFIX_UNIT
            total = suffix[:, :1] + hi[:, first:first + 1].astype(F32)
            ls = ls - total
        w = jnp.concatenate(ws, axis=1)
        if mask is not None:
            w = jnp.where(mask, w, 0.0)
        acc = acc + jnp.dot(w.astype(BF16), vs, preferred_element_type=F32)
        return acc, ls

    row = lax.broadcasted_iota(jnp.int32, (2 * tq, tq), 0)
    col = lax.broadcasted_iota(jnp.int32, (2 * tq, tq), 1)
    causal = col < jnp.where(row >= tq, row - tq, row)
    acc = jnp.zeros((2 * tq, LANES), F32)
    ls = jnp.zeros((2 * tq, LANES), F32)
    acc, ls = strip(pl.multiple_of(i * tq, tq), tq, acc, ls, causal)

    n_older = i * (tq // SUFFIX_UNIT)

    def more(c):
        return jnp.logical_and(c[0] < n_older, jnp.max(c[2]) > LS_FLOOR)

    def older_unit(c):
        n, acc, ls = c
        start = pl.multiple_of((n_older - 1 - n) * SUFFIX_UNIT, SUFFIX_UNIT)
        acc, ls = strip(start, SUFFIX_UNIT, acc, ls, None)
        return n + 1, acc, ls

    _, acc, ls = lax.while_loop(more, older_unit, (jnp.int32(0), acc, ls))
    o_ref[...] = jnp.where(lane < SB_HEAD_DIM, acc[:tq], acc[tq:]).astype(o_ref.dtype)


def _attention(q, k, v, bsz, seq):
    t = q.shape[0]
    tq = ATTN_TQ
    nq = seq // tq
    n_blk = SB_WIDTH // LANES
    qspec = pl.BlockSpec((tq, LANES), lambda b, h, i: (b * nq + i, h))
    kvspec = pl.BlockSpec((seq, LANES), lambda b, h, i: (b, h))
    return pl.pallas_call(
        _attn_kernel,
        grid=(bsz, n_blk, nq),
        in_specs=[qspec, kvspec, kvspec, _const_spec((SUFFIX_UNIT, SUFFIX_UNIT))],
        out_specs=qspec,
        out_shape=jax.ShapeDtypeStruct((t, SB_WIDTH), BF16),
        compiler_params=pltpu.CompilerParams(
            dimension_semantics=("parallel", "parallel", "arbitrary"),
            vmem_limit_bytes=VMEM_LIMIT),
        name="stickbreak_attention",
    )(q, k, v, _suffix_sum_matrix())


def _s5_kernel(u_ref, bre_ref, bim_ref, cre_ref, cim_ref, are_ref, aim_ref, d_ref,
               gw_ref, gb_ref, o_ref, sre_buf, sim_buf, cre_st, cim_st):
    bsz, tt, width = u_ref.shape
    n_chunk = bre_ref.shape[1] // LANES
    n_slab = n_chunk // SUBLANES
    n_ublk = width // LANES
    ch_per_ublk = n_chunk // n_ublk

    @pl.when(pl.program_id(0) == 0)
    def _():
        cre_st[...] = jnp.zeros_like(cre_st)
        cim_st[...] = jnp.zeros_like(cim_st)

    def slab_rows(c):
        return pl.ds(c % SUBLANES, tt, stride=SUBLANES)

    for b in range(bsz):
        ub = u_ref[b].astype(BF16)
        for blk in range(n_ublk):
            u_blk = ub[:, blk * LANES:(blk + 1) * LANES]
            for half in range(ch_per_ublk // 2):
                c0 = blk * ch_per_ublk + 2 * half
                cols = slice(c0 * LANES, (c0 + 2) * LANES)
                rows = slice(blk * LANES, (blk + 1) * LANES)
                for w_ref, buf in ((bre_ref, sre_buf), (bim_ref, sim_buf)):
                    val = jnp.dot(u_blk, w_ref[rows, cols], preferred_element_type=F32)
                    for e in range(2):
                        c = c0 + e
                        buf[b * n_slab + c // SUBLANES, slab_rows(c), :] = val[:, e * LANES:(e + 1) * LANES]

    a_re = [are_ref[s] for s in range(n_slab)]
    a_im = [aim_ref[s] for s in range(n_slab)]
    n_chain = bsz * n_slab

    def step(t, carry):
        base = pl.multiple_of(t * SUBLANES, SUBLANES)
        new = []
        for idx in range(n_chain):
            s_re, s_im = carry[idx]
            ar, ai = a_re[idx % n_slab], a_im[idx % n_slab]
            n_re = ar * s_re - ai * s_im + sre_buf[idx, pl.ds(base, SUBLANES), :]
            n_im = ar * s_im + ai * s_re + sim_buf[idx, pl.ds(base, SUBLANES), :]
            sre_buf[idx, pl.ds(base, SUBLANES), :] = n_re
            sim_buf[idx, pl.ds(base, SUBLANES), :] = n_im
            new.append((n_re, n_im))
        return tuple(new)

    init = tuple((cre_st[idx], cim_st[idx]) for idx in range(n_chain))
    final = lax.fori_loop(0, tt, step, init, unroll=8)
    for idx in range(n_chain):
        cre_st[idx] = final[idx][0]
        cim_st[idx] = final[idx][1]

    for b in range(bsz):
        ys = []
        for blk in range(n_ublk):
            acc = jnp.zeros((tt, LANES), F32)
            for half in range(ch_per_ublk // 2):
                c0 = blk * ch_per_ublk + 2 * half
                rows = slice(c0 * LANES, (c0 + 2) * LANES)
                cols = slice(blk * LANES, (blk + 1) * LANES)

                def chunk_pair(buf):
                    parts = [buf[b * n_slab + (c0 + e) // SUBLANES, slab_rows(c0 + e), :]
                             for e in range(2)]
                    return jnp.concatenate(parts, axis=1).astype(BF16)

                acc = acc + jnp.dot(chunk_pair(sre_buf), cre_ref[rows, cols], preferred_element_type=F32)
                acc = acc - jnp.dot(chunk_pair(sim_buf), cim_ref[rows, cols], preferred_element_type=F32)
            ys.append(acc)
        y = jnp.concatenate(ys, axis=1) + d_ref[...] * u_ref[b]
        y = jax.nn.gelu(y)
        gate = jnp.dot(y.astype(BF16), gw_ref[...], preferred_element_type=F32) + gb_ref[...]
        o_ref[b] = (y * jax.nn.sigmoid(gate)).astype(o_ref.dtype)


def _block_diag(blocks):
    g, r, c = blocks.shape
    eye = jnp.eye(g, dtype=blocks.dtype)
    return (eye[:, None, :, None] * blocks[:, :, None, :]).reshape(g * r, g * c)


def _s5_branch(u3, lam_re, lam_im, log_dt, b_re, b_im, c_re, c_im, d_skip, glu_w, glu_b):
    bsz, seq, width = u3.shape
    lb_re, lb_im, bb_re, bb_im = _s5_discretize(lam_re, lam_im, log_dt, b_re, b_im)
    n_state = lb_re.size
    n_slab = n_state // (SUBLANES * LANES)
    bmat_re = _block_diag(bb_re.transpose(0, 2, 1)).astype(BF16)
    bmat_im = _block_diag(bb_im.transpose(0, 2, 1)).astype(BF16)
    cmat_re = _block_diag(c_re.transpose(0, 2, 1)).astype(BF16)
    cmat_im = _block_diag(c_im.transpose(0, 2, 1)).astype(BF16)
    a_re = lb_re.reshape(n_slab, SUBLANES, LANES)
    a_im = lb_im.reshape(n_slab, SUBLANES, LANES)
    tt = S5_TT
    tile = pl.BlockSpec((bsz, tt, width), lambda i: (0, i, 0))
    slab_buf = pltpu.VMEM((bsz * n_slab, tt * SUBLANES, LANES), F32)
    carry = pltpu.VMEM((bsz * n_slab, SUBLANES, LANES), F32)
    return pl.pallas_call(
        _s5_kernel,
        grid=(seq // tt,),
        in_specs=[tile, _const_spec(bmat_re.shape), _const_spec(bmat_im.shape),
                  _const_spec(cmat_re.shape), _const_spec(cmat_im.shape),
                  _const_spec(a_re.shape), _const_spec(a_im.shape),
                  _const_spec((1, width)), _const_spec(glu_w.shape), _const_spec((1, width))],
        out_specs=tile,
        out_shape=jax.ShapeDtypeStruct((bsz, seq, width), BF16),
        scratch_shapes=[slab_buf, slab_buf, carry, carry],
        compiler_params=pltpu.CompilerParams(dimension_semantics=("arbitrary",),
                                             vmem_limit_bytes=VMEM_LIMIT),
        name="s5_branch",
    )(u3, bmat_re, bmat_im, cmat_re, cmat_im, a_re, a_im,
      d_skip.reshape(1, width), glu_w.astype(BF16), glu_b.reshape(1, width))


def _merge_ln1_kernel(x_ref, oa_ref, ob_ref, ga_ref, gb_ref, wa_ref, wb_ref, wo_ref,
                      g_ref, b_ref, h_ref, *, alpha):
    pa = jnp.dot(oa_ref[...], wa_ref[...], preferred_element_type=F32)
    pb = jnp.dot(ob_ref[...], wb_ref[...], preferred_element_type=F32)
    merged = ga_ref[...].astype(F32) * pa + gb_ref[...].astype(F32) * pb
    r = alpha * x_ref[...] + jnp.dot(merged.astype(BF16), wo_ref[...], preferred_element_type=F32)
    h_ref[...] = _layer_norm(r, g_ref[...], b_ref[...])


def _merge_ln1(x2, o_a, o_b, sg_a, sg_b, w_a, w_b, w_out, ln_g, ln_b, alpha):
    t, d = x2.shape
    tm = ROW_TILE
    row = lambda n: pl.BlockSpec((tm, n), lambda i: (i, 0))
    return pl.pallas_call(
        functools.partial(_merge_ln1_kernel, alpha=alpha),
        grid=(t // tm,),
        in_specs=[row(d), row(o_a.shape[1]), row(o_b.shape[1]), row(d), row(d),
                  _const_spec(w_a.shape), _const_spec(w_b.shape), _const_spec(w_out.shape),
                  _const_spec((1, d)), _const_spec((1, d))],
        out_specs=row(d),
        out_shape=jax.ShapeDtypeStruct((t, d), F32),
        compiler_params=pltpu.CompilerParams(dimension_semantics=("parallel",),
                                             vmem_limit_bytes=VMEM_LIMIT),
        name="merge_ln1",
    )(x2, o_a, o_b, sg_a, sg_b, w_a, w_b, w_out, ln_g.reshape(1, d), ln_b.reshape(1, d))


def _ffn_ple_ln2_kernel(h_ref, p_ref, wg_ref, wu_ref, wd_ref, wpg_ref, wpp_ref,
                        g_ref, b_ref, o_ref, *, alpha, ff_chunk):
    h = h_ref[...]
    hb = h.astype(BF16)
    d_ff = wg_ref.shape[1]
    ffn = jnp.zeros(h.shape, F32)
    for c0 in range(0, d_ff, ff_chunk):
        gate = jnp.dot(hb, wg_ref[:, c0:c0 + ff_chunk], preferred_element_type=F32)
        up = jnp.dot(hb, wu_ref[:, c0:c0 + ff_chunk], preferred_element_type=F32)
        act = (jax.nn.silu(gate) * up).astype(BF16)
        ffn = ffn + jnp.dot(act, wd_ref[c0:c0 + ff_chunk, :], preferred_element_type=F32)
    ple_gate = jax.nn.sigmoid(jnp.dot(hb, wpg_ref[...], preferred_element_type=F32))
    ple = ple_gate * jnp.dot(p_ref[...].astype(BF16), wpp_ref[...], preferred_element_type=F32)
    o_ref[...] = _layer_norm(alpha * h + ffn + ple, g_ref[...], b_ref[...])


def _ffn_ple_ln2(h1, p2, w_gate, w_up, w_down, w_pg, w_pp, ln_g, ln_b, alpha):
    t, d = h1.shape
    tm = ROW_TILE
    d_ff = w_gate.shape[1]
    ff_chunk = d_ff // 2 if (d_ff // 2) % LANES == 0 else d_ff
    row = lambda n: pl.BlockSpec((tm, n), lambda i: (i, 0))
    return pl.pallas_call(
        functools.partial(_ffn_ple_ln2_kernel, alpha=alpha, ff_chunk=ff_chunk),
        grid=(t // tm,),
        in_specs=[row(d), row(p2.shape[1]),
                  _const_spec(w_gate.shape), _const_spec(w_up.shape), _const_spec(w_down.shape),
                  _const_spec(w_pg.shape), _const_spec(w_pp.shape),
                  _const_spec((1, d)), _const_spec((1, d))],
        out_specs=row(d),
        out_shape=jax.ShapeDtypeStruct((t, d), F32),
        compiler_params=pltpu.CompilerParams(dimension_semantics=("parallel",),
                                             vmem_limit_bytes=VMEM_LIMIT),
        name="ffn_ple_ln2",
    )(h1, p2, w_gate, w_up, w_down, w_pg, w_pp, ln_g.reshape(1, d), ln_b.reshape(1, d))


def kernel(x, p, w_in, ssm_lambda_re, ssm_lambda_im, ssm_log_dt, ssm_b_re, ssm_b_im, ssm_c_re, ssm_c_im, ssm_d, ssm_glu_w, ssm_glu_b, w_branch_a, w_branch_b, w_out, ln1_g, ln1_b, ffn_w_gate, ffn_w_up, ffn_w_down, ple_w_gate, ple_w_proj, ln2_g, ln2_b):
    bsz, seq, d = x.shape
    depth = w_in.shape[0]
    alpha = (2 * depth) ** 0.25
    ssm_w = ssm_d.shape[1]
    t = bsz * seq
    h = x.reshape(t, d)
    for i in range(depth):
        q, k, v, u, sg_a, sg_b = _in_proj(h, w_in[i].astype(BF16), ssm_w)
        o_a = _attention(q, k, v, bsz, seq)
        o_b = _s5_branch(u.reshape(bsz, seq, ssm_w), ssm_lambda_re[i], ssm_lambda_im[i], ssm_log_dt[i],
                         ssm_b_re[i], ssm_b_im[i], ssm_c_re[i], ssm_c_im[i],
                         ssm_d[i], ssm_glu_w[i], ssm_glu_b[i]).reshape(t, ssm_w)
        h = _merge_ln1(h, o_a, o_b, sg_a, sg_b, w_branch_a[i].astype(BF16), w_branch_b[i].astype(BF16),
                       w_out[i].astype(BF16), ln1_g[i], ln1_b[i], alpha)
        h = _ffn_ple_ln2(h, p[i].reshape(t, -1), ffn_w_gate[i].astype(BF16), ffn_w_up[i].astype(BF16),
                         ffn_w_down[i].astype(BF16), ple_w_gate[i].astype(BF16),
                         ple_w_proj[i].astype(BF16), ln2_g[i], ln2_b[i], alpha)
    return h.reshape(bsz, seq, d)
```

```python
import functools

import jax
import jax.numpy as jnp
from jax import lax
from jax.experimental import pallas as pl
from jax.experimental.pallas import tpu as pltpu

F32 = jnp.float32
BF16 = jnp.bfloat16

LN_EPS = 1e-5
SB_HEADS = 8
SB_HEAD_DIM = 64
SB_WIDTH = SB_HEADS * SB_HEAD_DIM
SSM_GROUP = 16
SSM_STATE = 64

LANES = 128
SUBLANES = 8
SUFFIX_UNIT = 2 * LANES
LOG2_E = 1.4426950408889634
SP_LINEAR_FROM = 126.0
LS_FLOOR = -200.0
VMEM_LIMIT = 56 * 1024 * 1024

ROW_TILE = 512
ATTN_TQ = 512
S5_SUB = 256


def _const_spec(shape):
    nd = len(shape)
    return pl.BlockSpec(shape, lambda *_: (0,) * nd, pipeline_mode=pl.Buffered(1))


def _layer_norm(r, g, b):
    mu = jnp.mean(r, axis=-1, keepdims=True)
    d = r - mu
    var = jnp.mean(d * d, axis=-1, keepdims=True)
    return d * lax.rsqrt(var + LN_EPS) * g + b


def _s5_discretize_kernel(lre_ref, lim_ref, ldt_ref, bre_ref, bim_ref,
                          lbre_ref, lbim_ref, bbre_ref, bbim_ref):
    lam_re = lre_ref[...]
    lam_im = lim_ref[...]
    dt = jnp.exp(ldt_ref[...])
    mag = jnp.exp(lam_re * dt)
    lb_re = mag * jnp.cos(lam_im * dt)
    lb_im = mag * jnp.sin(lam_im * dt)
    den = lam_re * lam_re + lam_im * lam_im
    n_re = lb_re - 1.0
    coef_re = (n_re * lam_re + lb_im * lam_im) / den
    coef_im = (lb_im * lam_re - n_re * lam_im) / den
    b_re = bre_ref[...]
    b_im = bim_ref[...]
    lbre_ref[...] = lb_re
    lbim_ref[...] = lb_im
    bbre_ref[...] = coef_re * b_re - coef_im * b_im
    bbim_ref[...] = coef_re * b_im + coef_im * b_re


def _s5_discretize(lam_re, lam_im, log_dt, b_re, b_im):
    g, p = lam_re.shape
    c = b_re.shape[-1]
    rep = lambda a: jnp.repeat(a, c, axis=1)
    ldt = jnp.broadcast_to(log_dt[:, None], (g, p * c))
    shp = jax.ShapeDtypeStruct((g, p * c), F32)
    lb_re, lb_im, bb_re, bb_im = pl.pallas_call(
        _s5_discretize_kernel, out_shape=(shp, shp, shp, shp), name="s5_discretize",
    )(rep(lam_re), rep(lam_im), ldt, b_re.reshape(g, p * c), b_im.reshape(g, p * c))
    return (lb_re[:, ::c], lb_im[:, ::c],
            bb_re.reshape(g, p, c), bb_im.reshape(g, p, c))


def _in_proj_s5_kernel(x_ref, w_ref, bre_ref, bim_ref, cre_ref, cim_ref, are_ref, aim_ref,
                       d_ref, gw_ref, gbias_ref,
                       q_ref, k_ref, v_ref, ga_ref, gb_ref, ob_ref,
                       u_buf, cre_st, cim_st, *slab_bufs, q_scale, tiles_per_seq):
    i = pl.program_id(0)
    tm, d_model = x_ref.shape
    width = u_buf.shape[2]
    n_chunk = bre_ref.shape[1] // LANES
    n_slab = n_chunk // SUBLANES
    n_ublk = width // LANES
    ch_per_ublk = n_chunk // n_ublk
    sub = S5_SUB
    n_sub = tm // sub

    @pl.when(i == 0)
    def _():
        u_buf[...] = jnp.zeros_like(u_buf)
        cre_st[...] = jnp.zeros_like(cre_st)
        cim_st[...] = jnp.zeros_like(cim_st)

    slot = i % 2
    sre_bufs, sim_bufs = slab_bufs[0::2], slab_bufs[1::2]

    def u_prev(s):
        return u_buf[1 - slot, s * sub:(s + 1) * sub, :]

    def slab_rows(c):
        return pl.ds(c % SUBLANES, sub, stride=SUBLANES)

    def drive(s):
        ub = u_prev(s).astype(BF16)
        for blk in range(n_ublk):
            u_blk = ub[:, blk * LANES:(blk + 1) * LANES]
            for half in range(ch_per_ublk // 2):
                c0 = blk * ch_per_ublk + 2 * half
                cols = slice(c0 * LANES, (c0 + 2) * LANES)
                rows = slice(blk * LANES, (blk + 1) * LANES)
                for b_ref, buf in ((bre_ref, sre_bufs[s]), (bim_ref, sim_bufs[s])):
                    val = jnp.dot(u_blk, b_ref[rows, cols], preferred_element_type=F32)
                    for e in range(2):
                        c = c0 + e
                        buf[c // SUBLANES, slab_rows(c), :] = val[:, e * LANES:(e + 1) * LANES]

    a_re = [are_ref[k] for k in range(n_slab)]
    a_im = [aim_ref[k] for k in range(n_slab)]

    def scan(s, carry):
        sre_buf, sim_buf = sre_bufs[s], sim_bufs[s]
        for t in range(sub):
            tok = pl.ds(t * SUBLANES, SUBLANES)
            new = []
            for idx in range(n_slab):
                s_re, s_im = carry[idx]
                n_re = a_re[idx] * s_re - a_im[idx] * s_im + sre_buf[idx, tok, :]
                n_im = a_re[idx] * s_im + a_im[idx] * s_re + sim_buf[idx, tok, :]
                sre_buf[idx, tok, :] = n_re
                sim_buf[idx, tok, :] = n_im
                new.append((n_re, n_im))
            carry = new
        return carry

    def readout(s):
        ys = []
        for blk in range(n_ublk):
            acc = jnp.zeros((sub, LANES), F32)
            for half in range(ch_per_ublk // 2):
                c0 = blk * ch_per_ublk + 2 * half
                rows = slice(c0 * LANES, (c0 + 2) * LANES)
                cols = slice(blk * LANES, (blk + 1) * LANES)

                def chunk_pair(buf):
                    parts = [buf[(c0 + e) // SUBLANES, slab_rows(c0 + e), :] for e in range(2)]
                    return jnp.concatenate(parts, axis=1).astype(BF16)

                acc = acc + jnp.dot(chunk_pair(sre_bufs[s]), cre_ref[rows, cols],
                                    preferred_element_type=F32)
                acc = acc - jnp.dot(chunk_pair(sim_bufs[s]), cim_ref[rows, cols],
                                    preferred_element_type=F32)
            ys.append(acc)
        y = jnp.concatenate(ys, axis=1) + d_ref[...] * u_prev(s)
        y = jax.nn.gelu(y)
        gate = jnp.dot(y.astype(BF16), gw_ref[...], preferred_element_type=F32) + gbias_ref[...]
        ob_ref[s * sub:(s + 1) * sub, :] = (y * jax.nn.sigmoid(gate)).astype(ob_ref.dtype)

    opens_sequence = (i + tiles_per_seq - 1) % tiles_per_seq == 0
    carry = [(jnp.where(opens_sequence, 0.0, cre_st[idx]), jnp.where(opens_sequence, 0.0, cim_st[idx]))
             for idx in range(n_slab)]
    xb = x_ref[...].astype(BF16)

    def mm(c0, n):
        return jnp.dot(xb, w_ref[:, c0:c0 + n], preferred_element_type=F32)

    w = SB_WIDTH
    g0 = 3 * w + width
    proj = [
        lambda: q_ref.__setitem__(Ellipsis, (mm(0, w) * q_scale).astype(BF16)),
        lambda: k_ref.__setitem__(Ellipsis, mm(w, w).astype(BF16)),
        lambda: v_ref.__setitem__(Ellipsis, mm(2 * w, w).astype(BF16)),
        lambda: ga_ref.__setitem__(Ellipsis, jax.nn.sigmoid(mm(g0, d_model)).astype(BF16)),
        lambda: gb_ref.__setitem__(Ellipsis, jax.nn.sigmoid(mm(g0 + d_model, d_model)).astype(BF16)),
    ]
    for s in range(n_sub):
        drive(s)
    per_sub = -(-len(proj) // n_sub)
    for s in range(n_sub):
        carry = scan(s, carry)
        for piece in proj[s * per_sub:(s + 1) * per_sub]:
            piece()
        readout(s)
    for idx in range(n_slab):
        cre_st[idx] = carry[idx][0]
        cim_st[idx] = carry[idx][1]
    u_buf[slot] = mm(3 * w, width)


def _block_diag(blocks):
    g, r, c = blocks.shape
    eye = jnp.eye(g, dtype=blocks.dtype)
    return (eye[:, None, :, None] * blocks[:, :, None, :]).reshape(g * r, g * c)


def _in_proj_s5(x2, seq, w_in_bf, lam_re, lam_im, log_dt, b_re, b_im, c_re, c_im, d_skip, glu_w, glu_b):
    t, d = x2.shape
    width = d_skip.shape[0]
    lb_re, lb_im, bb_re, bb_im = _s5_discretize(lam_re, lam_im, log_dt, b_re, b_im)
    n_slab = lb_re.size // (SUBLANES * LANES)
    bmat_re = _block_diag(bb_re.transpose(0, 2, 1)).astype(BF16)
    bmat_im = _block_diag(bb_im.transpose(0, 2, 1)).astype(BF16)
    cmat_re = _block_diag(c_re.transpose(0, 2, 1)).astype(BF16)
    cmat_im = _block_diag(c_im.transpose(0, 2, 1)).astype(BF16)
    a_re = lb_re.reshape(n_slab, SUBLANES, LANES)
    a_im = lb_im.reshape(n_slab, SUBLANES, LANES)
    tm = ROW_TILE
    n_tiles = t // tm
    cur = lambda n: pl.BlockSpec((tm, n), lambda i: (jnp.minimum(i, n_tiles - 1), 0))
    prev = lambda n: pl.BlockSpec((tm, n), lambda i: (jnp.maximum(i - 1, 0), 0))
    sds = lambda n, dt: jax.ShapeDtypeStruct((t, n), dt)
    consts = (w_in_bf, bmat_re, bmat_im, cmat_re, cmat_im, a_re, a_im,
              d_skip.reshape(1, width), glu_w.astype(BF16), glu_b.reshape(1, width))
    slab_buf = pltpu.VMEM((n_slab, S5_SUB * SUBLANES, LANES), F32)
    carry = pltpu.VMEM((n_slab, SUBLANES, LANES), F32)
    return pl.pallas_call(
        functools.partial(_in_proj_s5_kernel, q_scale=LOG2_E * SB_HEAD_DIM ** -0.5,
                          tiles_per_seq=seq // tm),
        grid=(n_tiles + 1,),
        in_specs=[cur(d)] + [_const_spec(c.shape) for c in consts],
        out_specs=[cur(SB_WIDTH), cur(SB_WIDTH), cur(SB_WIDTH), cur(d), cur(d), prev(width)],
        out_shape=[sds(SB_WIDTH, BF16), sds(SB_WIDTH, BF16), sds(SB_WIDTH, BF16),
                   sds(d, BF16), sds(d, BF16), sds(width, BF16)],
        scratch_shapes=[pltpu.VMEM((2, tm, width), F32), carry, carry]
                       + [slab_buf] * (2 * (tm // S5_SUB)),
        compiler_params=pltpu.CompilerParams(dimension_semantics=("arbitrary",),
                                             vmem_limit_bytes=VMEM_LIMIT),
        name="in_proj_s5",
    )(x2, *consts)


def _suffix_sum_matrix():
    j = lax.broadcasted_iota(jnp.int32, (SUFFIX_UNIT, SUFFIX_UNIT), 0)
    s = lax.broadcasted_iota(jnp.int32, (SUFFIX_UNIT, SUFFIX_UNIT), 1)
    return jnp.where(j > s, 1.0, 0.0).astype(BF16)


def _attn_kernel(q_ref, k_ref, v_ref, m_ref, o_ref):
    tq = q_ref.shape[0]
    i = pl.program_id(2)
    lane = lax.broadcasted_iota(jnp.int32, (1, LANES), 1)
    q2 = q_ref[...]
    zero = jnp.zeros_like(q2)
    qs = jnp.concatenate([jnp.where(lane < SB_HEAD_DIM, q2, zero),
                          jnp.where(lane >= SB_HEAD_DIM, q2, zero)], axis=0)
    m = m_ref[...]

    def masked_tail(x, mask):
        if mask is None:
            return x
        head = x.shape[1] - mask.shape[1]
        tail = jnp.where(mask, x[:, head:], 0.0)
        return tail if head == 0 else jnp.concatenate([x[:, :head], tail], axis=1)

    def strip(start, width, acc, ls, mask):
        ks = k_ref[pl.ds(start, width), :]
        vs = v_ref[pl.ds(start, width), :]
        z = lax.dot_general(qs, ks, (((1,), (1,)), ((), ())), preferred_element_type=F32)
        sp = jnp.maximum(z, jnp.log(1.0 + jnp.exp2(jnp.minimum(z, SP_LINEAR_FROM))) * LOG2_E)
        d = z - sp
        hi = masked_tail(sp, mask).astype(BF16)
        n_unit = width // SUFFIX_UNIT
        ws = [None] * n_unit
        for u in reversed(range(n_unit)):
            sl = slice(u * SUFFIX_UNIT, (u + 1) * SUFFIX_UNIT)
            suffix = jnp.dot(hi[:, sl], m, preferred_element_type=F32)
            ws[u] = jnp.exp2(d[:, sl] - suffix + jnp.concatenate([ls, ls], axis=1))
            first = u * SUFFIX_UNIT
            total = suffix[:, :1] + hi[:, first:first + 1].astype(F32)
            ls = ls - total
        w = masked_tail(jnp.concatenate(ws, axis=1), mask)
        acc = acc + jnp.dot(w.astype(BF16), vs, preferred_element_type=F32)
        return acc, ls

    row = lax.broadcasted_iota(jnp.int32, (2 * tq, tq), 0)
    col = lax.broadcasted_iota(jnp.int32, (2 * tq, tq), 1)
    causal = col < jnp.where(row >= tq, row - tq, row)
    zeros = jnp.zeros((2 * tq, LANES), F32)
    n_older = i * (tq // SUFFIX_UNIT)

    def diag_and_previous_unit():
        start = pl.multiple_of(i * tq - SUFFIX_UNIT, SUFFIX_UNIT)
        return strip(start, tq + SUFFIX_UNIT, zeros, zeros, causal)

    def diag_only():
        return strip(0, tq, zeros, zeros, causal)

    acc, ls = lax.cond(i > 0, diag_and_previous_unit, diag_only)

    def more(c):
        return jnp.logical_and(c[0] < n_older, jnp.max(c[2]) > LS_FLOOR)

    def older_unit(c):
        n, acc, ls = c
        start = pl.multiple_of((n_older - 1 - n) * SUFFIX_UNIT, SUFFIX_UNIT)
        acc, ls = strip(start, SUFFIX_UNIT, acc, ls, None)
        return n + 1, acc, ls

    _, acc, ls = lax.while_loop(more, older_unit, (jnp.minimum(i, 1), acc, ls))
    o_ref[...] = jnp.where(lane < SB_HEAD_DIM, acc[:tq], acc[tq:]).astype(o_ref.dtype)


def _attention(q, k, v, bsz, seq):
    t = q.shape[0]
    tq = ATTN_TQ
    nq = seq // tq
    n_blk = SB_WIDTH // LANES
    qspec = pl.BlockSpec((tq, LANES), lambda b, h, i: (b * nq + i, h))
    kvspec = pl.BlockSpec((seq, LANES), lambda b, h, i: (b, h))
    return pl.pallas_call(
        _attn_kernel,
        grid=(bsz, n_blk, nq),
        in_specs=[qspec, kvspec, kvspec, _const_spec((SUFFIX_UNIT, SUFFIX_UNIT))],
        out_specs=qspec,
        out_shape=jax.ShapeDtypeStruct((t, SB_WIDTH), BF16),
        compiler_params=pltpu.CompilerParams(
            dimension_semantics=("parallel", "parallel", "arbitrary"),
            vmem_limit_bytes=VMEM_LIMIT),
        name="stickbreak_attention",
    )(q, k, v, _suffix_sum_matrix())


def _merge_ln1_kernel(x_ref, oa_ref, ob_ref, ga_ref, gb_ref, wa_ref, wb_ref, wo_ref,
                      g_ref, b_ref, h_ref, *, alpha):
    pa = jnp.dot(oa_ref[...], wa_ref[...], preferred_element_type=F32)
    pb = jnp.dot(ob_ref[...], wb_ref[...], preferred_element_type=F32)
    merged = ga_ref[...].astype(F32) * pa + gb_ref[...].astype(F32) * pb
    r = alpha * x_ref[...] + jnp.dot(merged.astype(BF16), wo_ref[...], preferred_element_type=F32)
    h_ref[...] = _layer_norm(r, g_ref[...], b_ref[...])


def _merge_ln1(x2, o_a, o_b, sg_a, sg_b, w_a, w_b, w_out, ln_g, ln_b, alpha):
    t, d = x2.shape
    tm = ROW_TILE
    row = lambda n: pl.BlockSpec((tm, n), lambda i: (i, 0))
    return pl.pallas_call(
        functools.partial(_merge_ln1_kernel, alpha=alpha),
        grid=(t // tm,),
        in_specs=[row(d), row(o_a.shape[1]), row(o_b.shape[1]), row(d), row(d),
                  _const_spec(w_a.shape), _const_spec(w_b.shape), _const_spec(w_out.shape),
                  _const_spec((1, d)), _const_spec((1, d))],
        out_specs=row(d),
        out_shape=jax.ShapeDtypeStruct((t, d), F32),
        compiler_params=pltpu.CompilerParams(dimension_semantics=("parallel",),
                                             vmem_limit_bytes=VMEM_LIMIT),
        name="merge_ln1",
    )(x2, o_a, o_b, sg_a, sg_b, w_a, w_b, w_out, ln_g.reshape(1, d), ln_b.reshape(1, d))


def _ffn_ple_ln2_kernel(h_ref, p_ref, wg_ref, wu_ref, wd_ref, wpg_ref, wpp_ref,
                        g_ref, b_ref, o_ref, *, alpha, ff_chunk):
    h = h_ref[...]
    hb = h.astype(BF16)
    d_ff = wg_ref.shape[1]
    ffn = jnp.zeros(h.shape, F32)
    for c0 in range(0, d_ff, ff_chunk):
        gate = jnp.dot(hb, wg_ref[:, c0:c0 + ff_chunk], preferred_element_type=F32)
        up = jnp.dot(hb, wu_ref[:, c0:c0 + ff_chunk], preferred_element_type=F32)
        act = (jax.nn.silu(gate) * up).astype(BF16)
        ffn = ffn + jnp.dot(act, wd_ref[c0:c0 + ff_chunk, :], preferred_element_type=F32)
    ple_gate = jax.nn.sigmoid(jnp.dot(hb, wpg_ref[...], preferred_element_type=F32))
    ple = ple_gate * jnp.dot(p_ref[...].astype(BF16), wpp_ref[...], preferred_element_type=F32)
    o_ref[...] = _layer_norm(alpha * h + ffn + ple, g_ref[...], b_ref[...])


def _ffn_ple_ln2(h1, p2, w_gate, w_up, w_down, w_pg, w_pp, ln_g, ln_b, alpha):
    t, d = h1.shape
    tm = ROW_TILE
    d_ff = w_gate.shape[1]
    ff_chunk = d_ff // 2 if (d_ff // 2) % LANES == 0 else d_ff
    row = lambda n: pl.BlockSpec((tm, n), lambda i: (i, 0))
    return pl.pallas_call(
        functools.partial(_ffn_ple_ln2_kernel, alpha=alpha, ff_chunk=ff_chunk),
        grid=(t // tm,),
        in_specs=[row(d), row(p2.shape[1]),
                  _const_spec(w_gate.shape), _const_spec(w_up.shape), _const_spec(w_down.shape),
                  _const_spec(w_pg.shape), _const_spec(w_pp.shape),
                  _const_spec((1, d)), _const_spec((1, d))],
        out_specs=row(d),
        out_shape=jax.ShapeDtypeStruct((t, d), F32),
        compiler_params=pltpu.CompilerParams(dimension_semantics=("parallel",),
                                             vmem_limit_bytes=VMEM_LIMIT),
        name="ffn_ple_ln2",
    )(h1, p2, w_gate, w_up, w_down, w_pg, w_pp, ln_g.reshape(1, d), ln_b.reshape(1, d))


def kernel(x, p, w_in, ssm_lambda_re, ssm_lambda_im, ssm_log_dt, ssm_b_re, ssm_b_im, ssm_c_re, ssm_c_im, ssm_d, ssm_glu_w, ssm_glu_b, w_branch_a, w_branch_b, w_out, ln1_g, ln1_b, ffn_w_gate, ffn_w_up, ffn_w_down, ple_w_gate, ple_w_proj, ln2_g, ln2_b):
    bsz, seq, d = x.shape
    depth = w_in.shape[0]
    alpha = (2 * depth) ** 0.25
    t = bsz * seq
    h = x.reshape(t, d)
    for i in range(depth):
        q, k, v, sg_a, sg_b, o_b = _in_proj_s5(
            h, seq, w_in[i].astype(BF16), ssm_lambda_re[i], ssm_lambda_im[i], ssm_log_dt[i],
            ssm_b_re[i], ssm_b_im[i], ssm_c_re[i], ssm_c_im[i], ssm_d[i], ssm_glu_w[i], ssm_glu_b[i])
        o_a = _attention(q, k, v, bsz, seq)
        h = _merge_ln1(h, o_a, o_b, sg_a, sg_b, w_branch_a[i].astype(BF16), w_branch_b[i].astype(BF16),
                       w_out[i].astype(BF16), ln1_g[i], ln1_b[i], alpha)
        h = _ffn_ple_ln2(h, p[i].reshape(t, -1), ffn_w_gate[i].astype(BF16), ffn_w_up[i].astype(BF16),
                         ffn_w_down[i].astype(BF16), ple_w_gate[i].astype(BF16),
                         ple_w_proj[i].astype(BF16), ln2_g[i], ln2_b[i], alpha)
    return h.reshape(bsz, seq, d)
```

```python
import functools

import jax
import jax.numpy as jnp
from jax import lax
from jax.experimental import pallas as pl
from jax.experimental.pallas import tpu as pltpu

F32 = jnp.float32
BF16 = jnp.bfloat16

LN_EPS = 1e-5
SB_HEADS = 8
SB_HEAD_DIM = 64
SB_WIDTH = SB_HEADS * SB_HEAD_DIM
SSM_GROUP = 16
SSM_STATE = 64

LANES = 128
SUBLANES = 8
SUFFIX_UNIT = 2 * LANES
LOG2_E = 1.4426950408889634
SP_LINEAR_FROM = 126.0
LS_FLOOR = -200.0
VMEM_LIMIT = 56 * 1024 * 1024

ROW_TILE = 512
ATTN_TQ = 512
S5_SUB = 256


def _const_spec(shape):
    nd = len(shape)
    return pl.BlockSpec(shape, lambda *_: (0,) * nd, pipeline_mode=pl.Buffered(1))


def _layer_norm(r, g, b):
    mu = jnp.mean(r, axis=-1, keepdims=True)
    d = r - mu
    var = jnp.mean(d * d, axis=-1, keepdims=True)
    return d * lax.rsqrt(var + LN_EPS) * g + b


def _s5_discretize_kernel(lre_ref, lim_ref, ldt_ref, bre_ref, bim_ref,
                          lbre_ref, lbim_ref, bbre_ref, bbim_ref):
    lam_re = lre_ref[...]
    lam_im = lim_ref[...]
    dt = jnp.exp(ldt_ref[...])
    mag = jnp.exp(lam_re * dt)
    lb_re = mag * jnp.cos(lam_im * dt)
    lb_im = mag * jnp.sin(lam_im * dt)
    den = lam_re * lam_re + lam_im * lam_im
    n_re = lb_re - 1.0
    coef_re = (n_re * lam_re + lb_im * lam_im) / den
    coef_im = (lb_im * lam_re - n_re * lam_im) / den
    b_re = bre_ref[...]
    b_im = bim_ref[...]
    lbre_ref[...] = lb_re
    lbim_ref[...] = lb_im
    bbre_ref[...] = coef_re * b_re - coef_im * b_im
    bbim_ref[...] = coef_re * b_im + coef_im * b_re


def _s5_discretize(lam_re, lam_im, log_dt, b_re, b_im):
    g, p = lam_re.shape
    c = b_re.shape[-1]
    rep = lambda a: jnp.repeat(a, c, axis=1)
    ldt = jnp.broadcast_to(log_dt[:, None], (g, p * c))
    shp = jax.ShapeDtypeStruct((g, p * c), F32)
    lb_re, lb_im, bb_re, bb_im = pl.pallas_call(
        _s5_discretize_kernel, out_shape=(shp, shp, shp, shp), name="s5_discretize",
    )(rep(lam_re), rep(lam_im), ldt, b_re.reshape(g, p * c), b_im.reshape(g, p * c))
    return (lb_re[:, ::c], lb_im[:, ::c],
            bb_re.reshape(g, p, c), bb_im.reshape(g, p, c))


def _in_proj_s5_kernel(x_ref, w_ref, bre_ref, bim_ref, cre_ref, cim_ref, are_ref, aim_ref,
                       d_ref, gw_ref, gbias_ref,
                       q_ref, k_ref, v_ref, ga_ref, gb_ref, ob_ref,
                       u_buf, cre_st, cim_st, *slab_bufs, q_scale, tiles_per_seq):
    i = pl.program_id(0)
    tm, d_model = x_ref.shape
    width = u_buf.shape[2]
    n_chunk = bre_ref.shape[1] // LANES
    n_slab = n_chunk // SUBLANES
    n_ublk = width // LANES
    ch_per_ublk = n_chunk // n_ublk
    sub = S5_SUB
    n_sub = tm // sub

    @pl.when(i == 0)
    def _():
        u_buf[...] = jnp.zeros_like(u_buf)
        cre_st[...] = jnp.zeros_like(cre_st)
        cim_st[...] = jnp.zeros_like(cim_st)

    slot = i % 2
    sre_bufs, sim_bufs = slab_bufs[0::2], slab_bufs[1::2]

    def u_prev(s):
        return u_buf[1 - slot, s * sub:(s + 1) * sub, :]

    def slab_rows(c):
        return pl.ds(c % SUBLANES, sub, stride=SUBLANES)

    def drive(s):
        ub = u_prev(s).astype(BF16)
        for blk in range(n_ublk):
            u_blk = ub[:, blk * LANES:(blk + 1) * LANES]
            for half in range(ch_per_ublk // 2):
                c0 = blk * ch_per_ublk + 2 * half
                cols = slice(c0 * LANES, (c0 + 2) * LANES)
                rows = slice(blk * LANES, (blk + 1) * LANES)
                for b_ref, buf in ((bre_ref, sre_bufs[s]), (bim_ref, sim_bufs[s])):
                    val = jnp.dot(u_blk, b_ref[rows, cols], preferred_element_type=F32)
                    for e in range(2):
                        c = c0 + e
                        buf[c // SUBLANES, slab_rows(c), :] = val[:, e * LANES:(e + 1) * LANES]

    a_re = [are_ref[k] for k in range(n_slab)]
    a_im = [aim_ref[k] for k in range(n_slab)]

    def scan(s, carry):
        sre_buf, sim_buf = sre_bufs[s], sim_bufs[s]
        for t in range(sub):
            tok = pl.ds(t * SUBLANES, SUBLANES)
            new = []
            for idx in range(n_slab):
                s_re, s_im = carry[idx]
                n_re = a_re[idx] * s_re - a_im[idx] * s_im + sre_buf[idx, tok, :]
                n_im = a_re[idx] * s_im + a_im[idx] * s_re + sim_buf[idx, tok, :]
                sre_buf[idx, tok, :] = n_re
                sim_buf[idx, tok, :] = n_im
                new.append((n_re, n_im))
            carry = new
        return carry

    def readout(s):
        ys = []
        for blk in range(n_ublk):
            acc = jnp.zeros((sub, LANES), F32)
            for half in range(ch_per_ublk // 2):
                c0 = blk * ch_per_ublk + 2 * half
                rows = slice(c0 * LANES, (c0 + 2) * LANES)
                cols = slice(blk * LANES, (blk + 1) * LANES)

                def chunk_pair(buf):
                    parts = [buf[(c0 + e) // SUBLANES, slab_rows(c0 + e), :] for e in range(2)]
                    return jnp.concatenate(parts, axis=1).astype(BF16)

                acc = acc + jnp.dot(chunk_pair(sre_bufs[s]), cre_ref[rows, cols],
                                    preferred_element_type=F32)
                acc = acc - jnp.dot(chunk_pair(sim_bufs[s]), cim_ref[rows, cols],
                                    preferred_element_type=F32)
            ys.append(acc)
        y = jnp.concatenate(ys, axis=1) + d_ref[...] * u_prev(s)
        y = jax.nn.gelu(y)
        gate = jnp.dot(y.astype(BF16), gw_ref[...], preferred_element_type=F32) + gbias_ref[...]
        ob_ref[s * sub:(s + 1) * sub, :] = (y * jax.nn.sigmoid(gate)).astype(ob_ref.dtype)

    opens_sequence = (i + tiles_per_seq - 1) % tiles_per_seq == 0
    carry = [(jnp.where(opens_sequence, 0.0, cre_st[idx]), jnp.where(opens_sequence, 0.0, cim_st[idx]))
             for idx in range(n_slab)]
    xb = x_ref[...].astype(BF16)

    def mm(c0, n):
        return jnp.dot(xb, w_ref[:, c0:c0 + n], preferred_element_type=F32)

    w = SB_WIDTH
    g0 = 3 * w + width
    proj = [
        lambda: q_ref.__setitem__(Ellipsis, (mm(0, w) * q_scale).astype(BF16)),
        lambda: k_ref.__setitem__(Ellipsis, mm(w, w).astype(BF16)),
        lambda: v_ref.__setitem__(Ellipsis, mm(2 * w, w).astype(BF16)),
        lambda: ga_ref.__setitem__(Ellipsis, jax.nn.sigmoid(mm(g0, d_model)).astype(BF16)),
        lambda: gb_ref.__setitem__(Ellipsis, jax.nn.sigmoid(mm(g0 + d_model, d_model)).astype(BF16)),
    ]
    for s in range(n_sub):
        drive(s)
    per_sub = -(-len(proj) // n_sub)
    for s in range(n_sub):
        carry = scan(s, carry)
        for piece in proj[s * per_sub:(s + 1) * per_sub]:
            piece()
        readout(s)
    for idx in range(n_slab):
        cre_st[idx] = carry[idx][0]
        cim_st[idx] = carry[idx][1]
    u_buf[slot] = mm(3 * w, width)


def _block_diag(blocks, dtype):
    g, r, c = blocks.shape
    wide = blocks.transpose(1, 0, 2).reshape(r, g * c)
    row_g = lax.broadcasted_iota(jnp.int32, (g * r, g * c), 0) // r
    col_g = lax.broadcasted_iota(jnp.int32, (g * r, g * c), 1) // c
    return jnp.where(row_g == col_g, jnp.tile(wide, (g, 1)), 0.0).astype(dtype)


def _in_proj_s5(x2, seq, w_in_bf, lam_re, lam_im, log_dt, b_re, b_im, c_re, c_im, d_skip, glu_w, glu_b):
    t, d = x2.shape
    width = d_skip.shape[0]
    lb_re, lb_im, bb_re, bb_im = _s5_discretize(lam_re, lam_im, log_dt, b_re, b_im)
    n_slab = lb_re.size // (SUBLANES * LANES)
    bmat_re = _block_diag(bb_re.transpose(0, 2, 1), BF16)
    bmat_im = _block_diag(bb_im.transpose(0, 2, 1), BF16)
    cmat_re = _block_diag(c_re.transpose(0, 2, 1), BF16)
    cmat_im = _block_diag(c_im.transpose(0, 2, 1), BF16)
    a_re = lb_re.reshape(n_slab, SUBLANES, LANES)
    a_im = lb_im.reshape(n_slab, SUBLANES, LANES)
    tm = ROW_TILE
    n_tiles = t // tm
    cur = lambda n: pl.BlockSpec((tm, n), lambda i: (jnp.minimum(i, n_tiles - 1), 0))
    prev = lambda n: pl.BlockSpec((tm, n), lambda i: (jnp.maximum(i - 1, 0), 0))
    sds = lambda n, dt: jax.ShapeDtypeStruct((t, n), dt)
    consts = (w_in_bf, bmat_re, bmat_im, cmat_re, cmat_im, a_re, a_im,
              d_skip.reshape(1, width), glu_w.astype(BF16), glu_b.reshape(1, width))
    slab_buf = pltpu.VMEM((n_slab, S5_SUB * SUBLANES, LANES), F32)
    carry = pltpu.VMEM((n_slab, SUBLANES, LANES), F32)
    return pl.pallas_call(
        functools.partial(_in_proj_s5_kernel, q_scale=LOG2_E * SB_HEAD_DIM ** -0.5,
                          tiles_per_seq=seq // tm),
        grid=(n_tiles + 1,),
        in_specs=[cur(d)] + [_const_spec(c.shape) for c in consts],
        out_specs=[cur(SB_WIDTH), cur(SB_WIDTH), cur(SB_WIDTH), cur(d), cur(d), prev(width)],
        out_shape=[sds(SB_WIDTH, BF16), sds(SB_WIDTH, BF16), sds(SB_WIDTH, BF16),
                   sds(d, BF16), sds(d, BF16), sds(width, BF16)],
        scratch_shapes=[pltpu.VMEM((2, tm, width), F32), carry, carry]
                       + [slab_buf] * (2 * (tm // S5_SUB)),
        compiler_params=pltpu.CompilerParams(dimension_semantics=("arbitrary",),
                                             vmem_limit_bytes=VMEM_LIMIT),
        name="in_proj_s5",
    )(x2, *consts)


def _suffix_sum_matrix():
    j = lax.broadcasted_iota(jnp.int32, (SUFFIX_UNIT, SUFFIX_UNIT), 0)
    s = lax.broadcasted_iota(jnp.int32, (SUFFIX_UNIT, SUFFIX_UNIT), 1)
    return jnp.where(j > s, 1.0, 0.0).astype(BF16)


def _attn_kernel(q_ref, k_ref, v_ref, m_ref, vis_ref, o_ref):
    tq = q_ref.shape[0]
    blk = SUFFIX_UNIT
    n_blk = tq // blk
    i = pl.program_id(2)
    lane = lax.broadcasted_iota(jnp.int32, (1, LANES), 1)
    q2 = q_ref[...]
    zero = jnp.zeros_like(q2)
    heads = (jnp.where(lane < SB_HEAD_DIM, q2, zero), jnp.where(lane >= SB_HEAD_DIM, q2, zero))
    qs = jnp.concatenate([h[b * blk:(b + 1) * blk] for b in range(n_blk) for h in heads], axis=0)
    m = m_ref[...]

    def unit(start, acc, ls, r0, mask):
        def masked(x):
            if mask is None:
                return x
            top = x[:mask.shape[0]] * mask
            return top if mask.shape[0] == x.shape[0] else jnp.concatenate([top, x[mask.shape[0]:]], axis=0)

        ks = k_ref[pl.ds(start, blk), :]
        vs = v_ref[pl.ds(start, blk), :]
        z = lax.dot_general(qs[r0:], ks, (((1,), (1,)), ((), ())), preferred_element_type=F32)
        sp = jnp.maximum(z, jnp.log(1.0 + jnp.exp2(jnp.minimum(z, SP_LINEAR_FROM))) * LOG2_E)
        d = z - sp
        hi = masked(sp).astype(BF16)
        suffix = jnp.dot(hi, m, preferred_element_type=F32)
        ls_in = ls[r0:]
        w = masked(jnp.exp2(d - suffix + jnp.concatenate([ls_in, ls_in], axis=1)))
        acc_out = acc[r0:] + jnp.dot(w.astype(BF16), vs, preferred_element_type=F32)
        ls_out = ls_in - (suffix[:, :1] + hi[:, :1].astype(F32))
        if r0 == 0:
            return acc_out, ls_out
        return (jnp.concatenate([acc[:r0], acc_out], axis=0),
                jnp.concatenate([ls[:r0], ls_out], axis=0))

    def diagonal_units(acc, ls):
        for j in reversed(range(n_blk)):
            acc, ls = unit(pl.multiple_of(i * tq + j * blk, blk), acc, ls, 2 * j * blk, vis_ref[...])
        return acc, ls

    zeros = jnp.zeros((2 * tq, LANES), F32)
    n_older = i * n_blk

    def diag_and_previous_unit():
        acc, ls = diagonal_units(zeros, zeros)
        return unit(pl.multiple_of(i * tq - blk, blk), acc, ls, 0, None)

    acc, ls = lax.cond(i > 0, diag_and_previous_unit, lambda: diagonal_units(zeros, zeros))

    def more(c):
        return jnp.logical_and(c[0] < n_older, jnp.max(c[2]) > LS_FLOOR)

    def older_unit(c):
        n, acc, ls = c
        acc, ls = unit(pl.multiple_of((n_older - 1 - n) * blk, blk), acc, ls, 0, None)
        return n + 1, acc, ls

    _, acc, ls = lax.while_loop(more, older_unit, (jnp.minimum(i, 1), acc, ls))
    per_head = [jnp.concatenate([acc[(2 * b + h) * blk:(2 * b + h + 1) * blk] for b in range(n_blk)],
                                axis=0) for h in range(2)]
    o_ref[...] = jnp.where(lane < SB_HEAD_DIM, per_head[0], per_head[1]).astype(o_ref.dtype)


def _causal_visibility():
    r = lax.broadcasted_iota(jnp.int32, (2 * SUFFIX_UNIT, SUFFIX_UNIT), 0)
    col = lax.broadcasted_iota(jnp.int32, (2 * SUFFIX_UNIT, SUFFIX_UNIT), 1)
    return (col < r % SUFFIX_UNIT).astype(F32)


def _attention(q, k, v, bsz, seq):
    t = q.shape[0]
    tq = ATTN_TQ
    nq = seq // tq
    n_blk = SB_WIDTH // LANES
    qspec = pl.BlockSpec((tq, LANES), lambda b, h, i: (b * nq + i, h))
    kvspec = pl.BlockSpec((seq, LANES), lambda b, h, i: (b, h))
    return pl.pallas_call(
        _attn_kernel,
        grid=(bsz, n_blk, nq),
        in_specs=[qspec, kvspec, kvspec, _const_spec((SUFFIX_UNIT, SUFFIX_UNIT)),
                  _const_spec((2 * SUFFIX_UNIT, SUFFIX_UNIT))],
        out_specs=qspec,
        out_shape=jax.ShapeDtypeStruct((t, SB_WIDTH), BF16),
        compiler_params=pltpu.CompilerParams(
            dimension_semantics=("parallel", "parallel", "arbitrary"),
            vmem_limit_bytes=VMEM_LIMIT),
        name="stickbreak_attention",
    )(q, k, v, _suffix_sum_matrix(), _causal_visibility())


def _merge_ln1_kernel(x_ref, oa_ref, ob_ref, ga_ref, gb_ref, wa_ref, wb_ref, wo_ref,
                      g_ref, b_ref, h_ref, *, alpha):
    pa = jnp.dot(oa_ref[...], wa_ref[...], preferred_element_type=F32)
    pb = jnp.dot(ob_ref[...], wb_ref[...], preferred_element_type=F32)
    merged = ga_ref[...].astype(F32) * pa + gb_ref[...].astype(F32) * pb
    r = alpha * x_ref[...] + jnp.dot(merged.astype(BF16), wo_ref[...], preferred_element_type=F32)
    h_ref[...] = _layer_norm(r, g_ref[...], b_ref[...])


def _merge_ln1(x2, o_a, o_b, sg_a, sg_b, w_a, w_b, w_out, ln_g, ln_b, alpha):
    t, d = x2.shape
    tm = ROW_TILE
    row = lambda n: pl.BlockSpec((tm, n), lambda i: (i, 0))
    return pl.pallas_call(
        functools.partial(_merge_ln1_kernel, alpha=alpha),
        grid=(t // tm,),
        in_specs=[row(d), row(o_a.shape[1]), row(o_b.shape[1]), row(d), row(d),
                  _const_spec(w_a.shape), _const_spec(w_b.shape), _const_spec(w_out.shape),
                  _const_spec((1, d)), _const_spec((1, d))],
        out_specs=row(d),
        out_shape=jax.ShapeDtypeStruct((t, d), F32),
        compiler_params=pltpu.CompilerParams(dimension_semantics=("parallel",),
                                             vmem_limit_bytes=VMEM_LIMIT),
        name="merge_ln1",
    )(x2, o_a, o_b, sg_a, sg_b, w_a, w_b, w_out, ln_g.reshape(1, d), ln_b.reshape(1, d))


def _ffn_ple_ln2_kernel(h_ref, p_ref, wg_ref, wu_ref, wd_ref, wpg_ref, wpp_ref,
                        g_ref, b_ref, o_ref, *, alpha, ff_chunk):
    h = h_ref[...]
    hb = h.astype(BF16)
    d_ff = wg_ref.shape[1]
    ffn = jnp.zeros(h.shape, F32)
    for c0 in range(0, d_ff, ff_chunk):
        gate = jnp.dot(hb, wg_ref[:, c0:c0 + ff_chunk], preferred_element_type=F32)
        up = jnp.dot(hb, wu_ref[:, c0:c0 + ff_chunk], preferred_element_type=F32)
        act = (jax.nn.silu(gate) * up).astype(BF16)
        ffn = ffn + jnp.dot(act, wd_ref[c0:c0 + ff_chunk, :], preferred_element_type=F32)
    ple_gate = jax.nn.sigmoid(jnp.dot(hb, wpg_ref[...], preferred_element_type=F32))
    ple = ple_gate * jnp.dot(p_ref[...].astype(BF16), wpp_ref[...], preferred_element_type=F32)
    o_ref[...] = _layer_norm(alpha * h + ffn + ple, g_ref[...], b_ref[...])


def _ffn_ple_ln2(h1, p2, w_gate, w_up, w_down, w_pg, w_pp, ln_g, ln_b, alpha):
    t, d = h1.shape
    tm = ROW_TILE
    d_ff = w_gate.shape[1]
    ff_chunk = d_ff // 2 if (d_ff // 2) % LANES == 0 else d_ff
    row = lambda n: pl.BlockSpec((tm, n), lambda i: (i, 0))
    return pl.pallas_call(
        functools.partial(_ffn_ple_ln2_kernel, alpha=alpha, ff_chunk=ff_chunk),
        grid=(t // tm,),
        in_specs=[row(d), row(p2.shape[1]),
                  _const_spec(w_gate.shape), _const_spec(w_up.shape), _const_spec(w_down.shape),
                  _const_spec(w_pg.shape), _const_spec(w_pp.shape),
                  _const_spec((1, d)), _const_spec((1, d))],
        out_specs=row(d),
        out_shape=jax.ShapeDtypeStruct((t, d), F32),
        compiler_params=pltpu.CompilerParams(dimension_semantics=("parallel",),
                                             vmem_limit_bytes=VMEM_LIMIT),
        name="ffn_ple_ln2",
    )(h1, p2, w_gate, w_up, w_down, w_pg, w_pp, ln_g.reshape(1, d), ln_b.reshape(1, d))


def kernel(x, p, w_in, ssm_lambda_re, ssm_lambda_im, ssm_log_dt, ssm_b_re, ssm_b_im, ssm_c_re, ssm_c_im, ssm_d, ssm_glu_w, ssm_glu_b, w_branch_a, w_branch_b, w_out, ln1_g, ln1_b, ffn_w_gate, ffn_w_up, ffn_w_down, ple_w_gate, ple_w_proj, ln2_g, ln2_b):
    bsz, seq, d = x.shape
    depth = w_in.shape[0]
    alpha = (2 * depth) ** 0.25
    t = bsz * seq
    h = x.reshape(t, d)
    for i in range(depth):
        q, k, v, sg_a, sg_b, o_b = _in_proj_s5(
            h, seq, w_in[i].astype(BF16), ssm_lambda_re[i], ssm_lambda_im[i], ssm_log_dt[i],
            ssm_b_re[i], ssm_b_im[i], ssm_c_re[i], ssm_c_im[i], ssm_d[i], ssm_glu_w[i], ssm_glu_b[i])
        o_a = _attention(q, k, v, bsz, seq)
        h = _merge_ln1(h, o_a, o_b, sg_a, sg_b, w_branch_a[i].astype(BF16), w_branch_b[i].astype(BF16),
                       w_out[i].astype(BF16), ln1_g[i], ln1_b[i], alpha)
        h = _ffn_ple_ln2(h, p[i].reshape(t, -1), ffn_w_gate[i].astype(BF16), ffn_w_up[i].astype(BF16),
                         ffn_w_down[i].astype(BF16), ple_w_gate[i].astype(BF16),
                         ple_w_proj[i].astype(BF16), ln2_g[i], ln2_b[i], alpha)
    return h.reshape(bsz, seq, d)
```

```python
import functools

import jax
import jax.numpy as jnp
from jax import lax
from jax.experimental import pallas as pl
from jax.experimental.pallas import tpu as pltpu

F32 = jnp.float32
BF16 = jnp.bfloat16

LN_EPS = 1e-5
SB_HEADS = 8
SB_HEAD_DIM = 64
SB_WIDTH = SB_HEADS * SB_HEAD_DIM
SSM_GROUP = 16
SSM_STATE = 64

LANES = 128
SUBLANES = 8
SUFFIX_UNIT = 2 * LANES
LOG2_E = 1.4426950408889634
SP_LINEAR_FROM = 126.0
LS_FLOOR = -200.0
VMEM_LIMIT = 56 * 1024 * 1024

ROW_TILE = 512
ATTN_TQ = 512
S5_SUB = 512


def _const_spec(shape):
    nd = len(shape)
    return pl.BlockSpec(shape, lambda *_: (0,) * nd, pipeline_mode=pl.Buffered(1))


def _layer_norm(r, g, b):
    mu = jnp.mean(r, axis=-1, keepdims=True)
    d = r - mu
    var = jnp.mean(d * d, axis=-1, keepdims=True)
    return d * lax.rsqrt(var + LN_EPS) * g + b


def _s5_discretize_kernel(lre_ref, lim_ref, ldt_ref, bre_ref, bim_ref,
                          lbre_ref, lbim_ref, bbre_ref, bbim_ref):
    lam_re = lre_ref[...]
    lam_im = lim_ref[...]
    dt = jnp.exp(ldt_ref[...])
    mag = jnp.exp(lam_re * dt)
    lb_re = mag * jnp.cos(lam_im * dt)
    lb_im = mag * jnp.sin(lam_im * dt)
    den = lam_re * lam_re + lam_im * lam_im
    n_re = lb_re - 1.0
    coef_re = (n_re * lam_re + lb_im * lam_im) / den
    coef_im = (lb_im * lam_re - n_re * lam_im) / den
    b_re = bre_ref[...]
    b_im = bim_ref[...]
    lbre_ref[...] = lb_re
    lbim_ref[...] = lb_im
    bbre_ref[...] = coef_re * b_re - coef_im * b_im
    bbim_ref[...] = coef_re * b_im + coef_im * b_re


def _s5_discretize(lam_re, lam_im, log_dt, b_re, b_im):
    g, p = lam_re.shape
    c = b_re.shape[-1]
    rep = lambda a: jnp.repeat(a, c, axis=1)
    ldt = jnp.broadcast_to(log_dt[:, None], (g, p * c))
    shp = jax.ShapeDtypeStruct((g, p * c), F32)
    lb_re, lb_im, bb_re, bb_im = pl.pallas_call(
        _s5_discretize_kernel, out_shape=(shp, shp, shp, shp), name="s5_discretize",
    )(rep(lam_re), rep(lam_im), ldt, b_re.reshape(g, p * c), b_im.reshape(g, p * c))
    return (lb_re[:, ::c], lb_im[:, ::c],
            bb_re.reshape(g, p, c), bb_im.reshape(g, p, c))


def _in_proj_s5_kernel(x_ref, w_ref, bre_ref, bim_ref, cre_ref, cim_ref, are_ref, aim_ref,
                       d_ref, gw_ref, gbias_ref,
                       q_ref, k_ref, v_ref, ga_ref, gb_ref, ob_ref,
                       u_buf, cre_st, cim_st, *slab_bufs, q_scale, tiles_per_seq):
    i = pl.program_id(0)
    tm, d_model = x_ref.shape
    width = u_buf.shape[2]
    n_chunk = bre_ref.shape[1] // LANES
    n_slab = n_chunk // SUBLANES
    n_ublk = width // LANES
    ch_per_ublk = n_chunk // n_ublk
    sub = S5_SUB
    assert sub == tm, "the source order below is written for one S5 sub-tile per row tile"

    @pl.when(i == 0)
    def _():
        u_buf[...] = jnp.zeros_like(u_buf)
        cre_st[...] = jnp.zeros_like(cre_st)
        cim_st[...] = jnp.zeros_like(cim_st)

    slot = i % 2
    sre_bufs, sim_bufs = slab_bufs[0::2], slab_bufs[1::2]

    def u_prev(s):
        return u_buf[1 - slot, s * sub:(s + 1) * sub, :]

    def slab_rows(c):
        return pl.ds(c % SUBLANES, sub, stride=SUBLANES)

    def drive(s):
        ub = u_prev(s).astype(BF16)
        for blk in range(n_ublk):
            u_blk = ub[:, blk * LANES:(blk + 1) * LANES]
            for half in range(ch_per_ublk // 2):
                c0 = blk * ch_per_ublk + 2 * half
                cols = slice(c0 * LANES, (c0 + 2) * LANES)
                rows = slice(blk * LANES, (blk + 1) * LANES)
                for b_ref, buf in ((bre_ref, sre_bufs[s]), (bim_ref, sim_bufs[s])):
                    val = jnp.dot(u_blk, b_ref[rows, cols], preferred_element_type=F32)
                    for e in range(2):
                        c = c0 + e
                        buf[c // SUBLANES, slab_rows(c), :] = val[:, e * LANES:(e + 1) * LANES]

    a_re = [are_ref[k] for k in range(n_slab)]
    a_im = [aim_ref[k] for k in range(n_slab)]

    def scan(s, carry):
        sre_buf, sim_buf = sre_bufs[s], sim_bufs[s]
        for t in range(sub):
            tok = pl.ds(t * SUBLANES, SUBLANES)
            new = []
            for idx in range(n_slab):
                s_re, s_im = carry[idx]
                n_re = a_re[idx] * s_re - a_im[idx] * s_im + sre_buf[idx, tok, :]
                n_im = a_re[idx] * s_im + a_im[idx] * s_re + sim_buf[idx, tok, :]
                sre_buf[idx, tok, :] = n_re
                sim_buf[idx, tok, :] = n_im
                new.append((n_re, n_im))
            carry = new
        return carry

    def readout(s):
        ys = []
        for blk in range(n_ublk):
            acc = jnp.zeros((sub, LANES), F32)
            for half in range(ch_per_ublk // 2):
                c0 = blk * ch_per_ublk + 2 * half
                rows = slice(c0 * LANES, (c0 + 2) * LANES)
                cols = slice(blk * LANES, (blk + 1) * LANES)

                def chunk_pair(buf):
                    parts = [buf[(c0 + e) // SUBLANES, slab_rows(c0 + e), :] for e in range(2)]
                    return jnp.concatenate(parts, axis=1).astype(BF16)

                acc = acc + jnp.dot(chunk_pair(sre_bufs[s]), cre_ref[rows, cols],
                                    preferred_element_type=F32)
                acc = acc - jnp.dot(chunk_pair(sim_bufs[s]), cim_ref[rows, cols],
                                    preferred_element_type=F32)
            ys.append(acc)
        y = jnp.concatenate(ys, axis=1) + d_ref[...] * u_prev(s)
        y = jax.nn.gelu(y)
        gate = jnp.dot(y.astype(BF16), gw_ref[...], preferred_element_type=F32) + gbias_ref[...]
        ob_ref[s * sub:(s + 1) * sub, :] = (y * jax.nn.sigmoid(gate)).astype(ob_ref.dtype)

    opens_sequence = (i + tiles_per_seq - 1) % tiles_per_seq == 0
    carry = [(jnp.where(opens_sequence, 0.0, cre_st[idx]), jnp.where(opens_sequence, 0.0, cim_st[idx]))
             for idx in range(n_slab)]
    xb = x_ref[...].astype(BF16)
    w = SB_WIDTH
    g0 = 3 * w + width
    half = d_model // 2

    def project(c0, n):
        return jnp.dot(xb, w_ref[:, c0:c0 + n], preferred_element_type=F32)

    def gate(out_ref, c0, lo):
        out_ref[:, lo:lo + half] = jax.nn.sigmoid(project(c0 + lo, half)).astype(BF16)

    q_ref[...] = (project(0, w) * q_scale).astype(BF16)
    drive(0)
    carry = scan(0, carry)
    k_ref[...] = project(w, w).astype(BF16)
    v_ref[...] = project(2 * w, w).astype(BF16)
    gate(ga_ref, g0, 0)
    gate(ga_ref, g0, half)
    gate(gb_ref, g0 + d_model, 0)
    readout(0)
    gate(gb_ref, g0 + d_model, half)
    for idx in range(n_slab):
        cre_st[idx] = carry[idx][0]
        cim_st[idx] = carry[idx][1]
    u_buf[slot] = project(3 * w, width)


def _block_diag(blocks, dtype):
    g, r, c = blocks.shape
    wide = blocks.transpose(1, 0, 2).reshape(r, g * c)
    row_g = lax.broadcasted_iota(jnp.int32, (g * r, g * c), 0) // r
    col_g = lax.broadcasted_iota(jnp.int32, (g * r, g * c), 1) // c
    return jnp.where(row_g == col_g, jnp.tile(wide, (g, 1)), 0.0).astype(dtype)


def _in_proj_s5(x2, seq, w_in_bf, lam_re, lam_im, log_dt, b_re, b_im, c_re, c_im, d_skip, glu_w, glu_b):
    t, d = x2.shape
    width = d_skip.shape[0]
    lb_re, lb_im, bb_re, bb_im = _s5_discretize(lam_re, lam_im, log_dt, b_re, b_im)
    n_slab = lb_re.size // (SUBLANES * LANES)
    bmat_re = _block_diag(bb_re.transpose(0, 2, 1), BF16)
    bmat_im = _block_diag(bb_im.transpose(0, 2, 1), BF16)
    cmat_re = _block_diag(c_re.transpose(0, 2, 1), BF16)
    cmat_im = _block_diag(c_im.transpose(0, 2, 1), BF16)
    a_re = lb_re.reshape(n_slab, SUBLANES, LANES)
    a_im = lb_im.reshape(n_slab, SUBLANES, LANES)
    tm = ROW_TILE
    n_tiles = t // tm
    cur = lambda n: pl.BlockSpec((tm, n), lambda i: (jnp.minimum(i, n_tiles - 1), 0))
    prev = lambda n: pl.BlockSpec((tm, n), lambda i: (jnp.maximum(i - 1, 0), 0))
    sds = lambda n, dt: jax.ShapeDtypeStruct((t, n), dt)
    consts = (w_in_bf, bmat_re, bmat_im, cmat_re, cmat_im, a_re, a_im,
              d_skip.reshape(1, width), glu_w.astype(BF16), glu_b.reshape(1, width))
    slab_buf = pltpu.VMEM((n_slab, S5_SUB * SUBLANES, LANES), F32)
    carry = pltpu.VMEM((n_slab, SUBLANES, LANES), F32)
    return pl.pallas_call(
        functools.partial(_in_proj_s5_kernel, q_scale=LOG2_E * SB_HEAD_DIM ** -0.5,
                          tiles_per_seq=seq // tm),
        grid=(n_tiles + 1,),
        in_specs=[cur(d)] + [_const_spec(c.shape) for c in consts],
        out_specs=[cur(SB_WIDTH), cur(SB_WIDTH), cur(SB_WIDTH), cur(d), cur(d), prev(width)],
        out_shape=[sds(SB_WIDTH, BF16), sds(SB_WIDTH, BF16), sds(SB_WIDTH, BF16),
                   sds(d, BF16), sds(d, BF16), sds(width, BF16)],
        scratch_shapes=[pltpu.VMEM((2, tm, width), F32), carry, carry]
                       + [slab_buf] * (2 * (tm // S5_SUB)),
        compiler_params=pltpu.CompilerParams(dimension_semantics=("arbitrary",),
                                             vmem_limit_bytes=VMEM_LIMIT),
        name="in_proj_s5",
    )(x2, *consts)


def _suffix_sum_matrix():
    j = lax.broadcasted_iota(jnp.int32, (SUFFIX_UNIT, SUFFIX_UNIT), 0)
    s = lax.broadcasted_iota(jnp.int32, (SUFFIX_UNIT, SUFFIX_UNIT), 1)
    return jnp.where(j > s, 1.0, 0.0).astype(BF16)


def _attn_kernel(q_ref, k_ref, v_ref, m_ref, vis_ref, o_ref):
    tq = q_ref.shape[0]
    blk = SUFFIX_UNIT
    n_blk = tq // blk
    i = pl.program_id(2)
    lane = lax.broadcasted_iota(jnp.int32, (1, LANES), 1)
    q2 = q_ref[...]
    zero = jnp.zeros_like(q2)
    heads = (jnp.where(lane < SB_HEAD_DIM, q2, zero), jnp.where(lane >= SB_HEAD_DIM, q2, zero))
    qs = jnp.concatenate([h[b * blk:(b + 1) * blk] for b in range(n_blk) for h in heads], axis=0)
    m = m_ref[...]

    def unit(start, acc, ls, r0, mask):
        def masked(x):
            if mask is None:
                return x
            top = x[:mask.shape[0]] * mask
            return top if mask.shape[0] == x.shape[0] else jnp.concatenate([top, x[mask.shape[0]:]], axis=0)

        ks = k_ref[pl.ds(start, blk), :]
        vs = v_ref[pl.ds(start, blk), :]
        z = lax.dot_general(qs[r0:], ks, (((1,), (1,)), ((), ())), preferred_element_type=F32)
        sp = jnp.maximum(z, jnp.log(1.0 + jnp.exp2(jnp.minimum(z, SP_LINEAR_FROM))) * LOG2_E)
        d = z - sp
        hi = masked(sp).astype(BF16)
        suffix = jnp.dot(hi, m, preferred_element_type=F32)
        ls_in = ls[r0:]
        w = masked(jnp.exp2(d - suffix + jnp.concatenate([ls_in, ls_in], axis=1)))
        acc_out = acc[r0:] + jnp.dot(w.astype(BF16), vs, preferred_element_type=F32)
        ls_out = ls_in - (suffix[:, :1] + hi[:, :1].astype(F32))
        if r0 == 0:
            return acc_out, ls_out
        return (jnp.concatenate([acc[:r0], acc_out], axis=0),
                jnp.concatenate([ls[:r0], ls_out], axis=0))

    def diagonal_units(acc, ls):
        for j in reversed(range(n_blk)):
            acc, ls = unit(pl.multiple_of(i * tq + j * blk, blk), acc, ls, 2 * j * blk, vis_ref[...])
        return acc, ls

    zeros = jnp.zeros((2 * tq, LANES), F32)
    n_older = i * n_blk

    def diag_and_previous_unit():
        acc, ls = diagonal_units(zeros, zeros)
        return unit(pl.multiple_of(i * tq - blk, blk), acc, ls, 0, None)

    acc, ls = lax.cond(i > 0, diag_and_previous_unit, lambda: diagonal_units(zeros, zeros))

    def more(c):
        return jnp.logical_and(c[0] < n_older, jnp.max(c[2]) > LS_FLOOR)

    def older_unit(c):
        n, acc, ls = c
        acc, ls = unit(pl.multiple_of((n_older - 1 - n) * blk, blk), acc, ls, 0, None)
        return n + 1, acc, ls

    _, acc, ls = lax.while_loop(more, older_unit, (jnp.minimum(i, 1), acc, ls))
    per_head = [jnp.concatenate([acc[(2 * b + h) * blk:(2 * b + h + 1) * blk] for b in range(n_blk)],
                                axis=0) for h in range(2)]
    o_ref[...] = jnp.where(lane < SB_HEAD_DIM, per_head[0], per_head[1]).astype(o_ref.dtype)


def _causal_visibility():
    r = lax.broadcasted_iota(jnp.int32, (2 * SUFFIX_UNIT, SUFFIX_UNIT), 0)
    col = lax.broadcasted_iota(jnp.int32, (2 * SUFFIX_UNIT, SUFFIX_UNIT), 1)
    return (col < r % SUFFIX_UNIT).astype(F32)


def _attention(q, k, v, bsz, seq):
    t = q.shape[0]
    tq = ATTN_TQ
    nq = seq // tq
    n_blk = SB_WIDTH // LANES
    qspec = pl.BlockSpec((tq, LANES), lambda b, h, i: (b * nq + i, h))
    kvspec = pl.BlockSpec((seq, LANES), lambda b, h, i: (b, h))
    return pl.pallas_call(
        _attn_kernel,
        grid=(bsz, n_blk, nq),
        in_specs=[qspec, kvspec, kvspec, _const_spec((SUFFIX_UNIT, SUFFIX_UNIT)),
                  _const_spec((2 * SUFFIX_UNIT, SUFFIX_UNIT))],
        out_specs=qspec,
        out_shape=jax.ShapeDtypeStruct((t, SB_WIDTH), BF16),
        compiler_params=pltpu.CompilerParams(
            dimension_semantics=("parallel", "parallel", "arbitrary"),
            vmem_limit_bytes=VMEM_LIMIT),
        name="stickbreak_attention",
    )(q, k, v, _suffix_sum_matrix(), _causal_visibility())


def _merge_ln1_kernel(x_ref, oa_ref, ob_ref, ga_ref, gb_ref, wa_ref, wb_ref, wo_ref,
                      g_ref, b_ref, h_ref, *, alpha):
    sub = x_ref.shape[0] // 2
    for r0 in (0, sub):
        rows = slice(r0, r0 + sub)
        pa = jnp.dot(oa_ref[rows, :], wa_ref[...], preferred_element_type=F32)
        pb = jnp.dot(ob_ref[rows, :], wb_ref[...], preferred_element_type=F32)
        merged = ga_ref[rows, :].astype(F32) * pa + gb_ref[rows, :].astype(F32) * pb
        r = alpha * x_ref[rows, :] + jnp.dot(merged.astype(BF16), wo_ref[...],
                                             preferred_element_type=F32)
        h_ref[rows, :] = _layer_norm(r, g_ref[...], b_ref[...])


def _merge_ln1(x2, o_a, o_b, sg_a, sg_b, w_a, w_b, w_out, ln_g, ln_b, alpha):
    t, d = x2.shape
    tm = ROW_TILE
    row = lambda n: pl.BlockSpec((tm, n), lambda i: (i, 0))
    return pl.pallas_call(
        functools.partial(_merge_ln1_kernel, alpha=alpha),
        grid=(t // tm,),
        in_specs=[row(d), row(o_a.shape[1]), row(o_b.shape[1]), row(d), row(d),
                  _const_spec(w_a.shape), _const_spec(w_b.shape), _const_spec(w_out.shape),
                  _const_spec((1, d)), _const_spec((1, d))],
        out_specs=row(d),
        out_shape=jax.ShapeDtypeStruct((t, d), F32),
        compiler_params=pltpu.CompilerParams(dimension_semantics=("parallel",),
                                             vmem_limit_bytes=VMEM_LIMIT),
        name="merge_ln1",
    )(x2, o_a, o_b, sg_a, sg_b, w_a, w_b, w_out, ln_g.reshape(1, d), ln_b.reshape(1, d))


def _ffn_ple_ln2_kernel(h_ref, p_ref, wg_ref, wu_ref, wd_ref, wpg_ref, wpp_ref,
                        g_ref, b_ref, o_ref, *, alpha, ff_chunk):
    h = h_ref[...]
    hb = h.astype(BF16)
    d_ff = wg_ref.shape[1]
    ffn = jnp.zeros(h.shape, F32)
    for c0 in range(0, d_ff, ff_chunk):
        gate = jnp.dot(hb, wg_ref[:, c0:c0 + ff_chunk], preferred_element_type=F32)
        up = jnp.dot(hb, wu_ref[:, c0:c0 + ff_chunk], preferred_element_type=F32)
        act = (jax.nn.silu(gate) * up).astype(BF16)
        ffn = ffn + jnp.dot(act, wd_ref[c0:c0 + ff_chunk, :], preferred_element_type=F32)
    ple_gate = jax.nn.sigmoid(jnp.dot(hb, wpg_ref[...], preferred_element_type=F32))
    ple = ple_gate * jnp.dot(p_ref[...].astype(BF16), wpp_ref[...], preferred_element_type=F32)
    o_ref[...] = _layer_norm(alpha * h + ffn + ple, g_ref[...], b_ref[...])


def _ffn_ple_ln2(h1, p2, w_gate, w_up, w_down, w_pg, w_pp, ln_g, ln_b, alpha):
    t, d = h1.shape
    tm = ROW_TILE
    d_ff = w_gate.shape[1]
    ff_chunk = 2 * LANES if d_ff % (2 * LANES) == 0 else d_ff
    row = lambda n: pl.BlockSpec((tm, n), lambda i: (i, 0))
    return pl.pallas_call(
        functools.partial(_ffn_ple_ln2_kernel, alpha=alpha, ff_chunk=ff_chunk),
        grid=(t // tm,),
        in_specs=[row(d), row(p2.shape[1]),
                  _const_spec(w_gate.shape), _const_spec(w_up.shape), _const_spec(w_down.shape),
                  _const_spec(w_pg.shape), _const_spec(w_pp.shape),
                  _const_spec((1, d)), _const_spec((1, d))],
        out_specs=row(d),
        out_shape=jax.ShapeDtypeStruct((t, d), F32),
        compiler_params=pltpu.CompilerParams(dimension_semantics=("parallel",),
                                             vmem_limit_bytes=VMEM_LIMIT),
        name="ffn_ple_ln2",
    )(h1, p2, w_gate, w_up, w_down, w_pg, w_pp, ln_g.reshape(1, d), ln_b.reshape(1, d))


def kernel(x, p, w_in, ssm_lambda_re, ssm_lambda_im, ssm_log_dt, ssm_b_re, ssm_b_im, ssm_c_re, ssm_c_im, ssm_d, ssm_glu_w, ssm_glu_b, w_branch_a, w_branch_b, w_out, ln1_g, ln1_b, ffn_w_gate, ffn_w_up, ffn_w_down, ple_w_gate, ple_w_proj, ln2_g, ln2_b):
    bsz, seq, d = x.shape
    depth = w_in.shape[0]
    alpha = (2 * depth) ** 0.25
    t = bsz * seq
    h = x.reshape(t, d)
    for i in range(depth):
        q, k, v, sg_a, sg_b, o_b = _in_proj_s5(
            h, seq, w_in[i].astype(BF16), ssm_lambda_re[i], ssm_lambda_im[i], ssm_log_dt[i],
            ssm_b_re[i], ssm_b_im[i], ssm_c_re[i], ssm_c_im[i], ssm_d[i], ssm_glu_w[i], ssm_glu_b[i])
        o_a = _attention(q, k, v, bsz, seq)
        h = _merge_ln1(h, o_a, o_b, sg_a, sg_b, w_branch_a[i].astype(BF16), w_branch_b[i].astype(BF16),
                       w_out[i].astype(BF16), ln1_g[i], ln1_b[i], alpha)
        h = _ffn_ple_ln2(h, p[i].reshape(t, -1), ffn_w_gate[i].astype(BF16), ffn_w_up[i].astype(BF16),
                         ffn_w_down[i].astype(BF16), ple_w_gate[i].astype(BF16),
                         ple_w_proj[i].astype(BF16), ln2_g[i], ln2_b[i], alpha)
    return h.reshape(bsz, seq, d)
```

```python
import functools

import jax
import jax.numpy as jnp
from jax import lax
from jax.experimental import pallas as pl
from jax.experimental.pallas import tpu as pltpu

F32 = jnp.float32
BF16 = jnp.bfloat16

LN_EPS = 1e-5
SB_HEADS = 8
SB_HEAD_DIM = 64
SB_WIDTH = SB_HEADS * SB_HEAD_DIM
SSM_GROUP = 16
SSM_STATE = 64

LANES = 128
SUBLANES = 8
SUFFIX_UNIT = 2 * LANES
LOG2_E = 1.4426950408889634
SP_LINEAR_FROM = 126.0
LS_FLOOR = -200.0
VMEM_LIMIT = 56 * 1024 * 1024

ROW_TILE = 512
ATTN_TQ = 512
S5_SUB = 512


def _const_spec(shape):
    nd = len(shape)
    return pl.BlockSpec(shape, lambda *_: (0,) * nd, pipeline_mode=pl.Buffered(1))


def _layer_norm(r, g, b):
    mu = jnp.mean(r, axis=-1, keepdims=True)
    d = r - mu
    var = jnp.mean(d * d, axis=-1, keepdims=True)
    return d * lax.rsqrt(var + LN_EPS) * g + b


def _s5_discretize_kernel(lre_ref, lim_ref, ldt_ref, bre_ref, bim_ref,
                          lbre_ref, lbim_ref, bbre_ref, bbim_ref):
    lam_re = lre_ref[...]
    lam_im = lim_ref[...]
    dt = jnp.exp(ldt_ref[...])
    mag = jnp.exp(lam_re * dt)
    lb_re = mag * jnp.cos(lam_im * dt)
    lb_im = mag * jnp.sin(lam_im * dt)
    den = lam_re * lam_re + lam_im * lam_im
    n_re = lb_re - 1.0
    coef_re = (n_re * lam_re + lb_im * lam_im) / den
    coef_im = (lb_im * lam_re - n_re * lam_im) / den
    b_re = bre_ref[...]
    b_im = bim_ref[...]
    lbre_ref[...] = lb_re
    lbim_ref[...] = lb_im
    bbre_ref[...] = coef_re * b_re - coef_im * b_im
    bbim_ref[...] = coef_re * b_im + coef_im * b_re


def _s5_discretize(lam_re, lam_im, log_dt, b_re, b_im):
    g, p = lam_re.shape
    c = b_re.shape[-1]
    rep = lambda a: jnp.repeat(a, c, axis=1)
    ldt = jnp.broadcast_to(log_dt[:, None], (g, p * c))
    shp = jax.ShapeDtypeStruct((g, p * c), F32)
    lb_re, lb_im, bb_re, bb_im = pl.pallas_call(
        _s5_discretize_kernel, out_shape=(shp, shp, shp, shp), name="s5_discretize",
    )(rep(lam_re), rep(lam_im), ldt, b_re.reshape(g, p * c), b_im.reshape(g, p * c))
    return (lb_re[:, ::c], lb_im[:, ::c],
            bb_re.reshape(g, p, c), bb_im.reshape(g, p, c))


def _in_proj_s5_kernel(x_ref, w_ref, bre_ref, bim_ref, cre_ref, cim_ref, are_ref, aim_ref,
                       d_ref, gw_ref, gbias_ref,
                       q_ref, k_ref, v_ref, ga_ref, gb_ref, ob_ref,
                       u_buf, cre_st, cim_st, *slab_bufs, q_scale, tiles_per_seq):
    i = pl.program_id(0)
    tm, d_model = x_ref.shape
    width = u_buf.shape[2]
    n_chunk = bre_ref.shape[1] // LANES
    n_slab = n_chunk // SUBLANES
    n_ublk = width // LANES
    ch_per_ublk = n_chunk // n_ublk
    sub = S5_SUB
    assert sub == tm, "the source order below is written for one S5 sub-tile per row tile"

    @pl.when(i == 0)
    def _():
        u_buf[...] = jnp.zeros_like(u_buf)
        cre_st[...] = jnp.zeros_like(cre_st)
        cim_st[...] = jnp.zeros_like(cim_st)

    slot = i % 2
    sre_bufs, sim_bufs = slab_bufs[0::2], slab_bufs[1::2]

    def u_prev(s):
        return u_buf[1 - slot, s * sub:(s + 1) * sub, :]

    def slab_rows(c):
        return pl.ds(c % SUBLANES, sub, stride=SUBLANES)

    def drive(s):
        ub = u_prev(s).astype(BF16)
        for blk in range(n_ublk):
            u_blk = ub[:, blk * LANES:(blk + 1) * LANES]
            for half in range(ch_per_ublk // 2):
                c0 = blk * ch_per_ublk + 2 * half
                cols = slice(c0 * LANES, (c0 + 2) * LANES)
                rows = slice(blk * LANES, (blk + 1) * LANES)
                for b_ref, buf in ((bre_ref, sre_bufs[s]), (bim_ref, sim_bufs[s])):
                    val = jnp.dot(u_blk, b_ref[rows, cols], preferred_element_type=F32)
                    for e in range(2):
                        c = c0 + e
                        buf[c // SUBLANES, slab_rows(c), :] = val[:, e * LANES:(e + 1) * LANES]

    a_re = [are_ref[k] for k in range(n_slab)]
    a_im = [aim_ref[k] for k in range(n_slab)]

    def scan(s, carry):
        sre_buf, sim_buf = sre_bufs[s], sim_bufs[s]
        for t in range(sub):
            tok = pl.ds(t * SUBLANES, SUBLANES)
            new = []
            for idx in range(n_slab):
                s_re, s_im = carry[idx]
                n_re = a_re[idx] * s_re - a_im[idx] * s_im + sre_buf[idx, tok, :]
                n_im = a_re[idx] * s_im + a_im[idx] * s_re + sim_buf[idx, tok, :]
                sre_buf[idx, tok, :] = n_re
                sim_buf[idx, tok, :] = n_im
                new.append((n_re, n_im))
            carry = new
        return carry

    def readout(s):
        ys = []
        for blk in range(n_ublk):
            acc = jnp.zeros((sub, LANES), F32)
            for half in range(ch_per_ublk // 2):
                c0 = blk * ch_per_ublk + 2 * half
                rows = slice(c0 * LANES, (c0 + 2) * LANES)
                cols = slice(blk * LANES, (blk + 1) * LANES)

                def chunk_pair(buf):
                    parts = [buf[(c0 + e) // SUBLANES, slab_rows(c0 + e), :] for e in range(2)]
                    return jnp.concatenate(parts, axis=1).astype(BF16)

                acc = acc + jnp.dot(chunk_pair(sre_bufs[s]), cre_ref[rows, cols],
                                    preferred_element_type=F32)
                acc = acc - jnp.dot(chunk_pair(sim_bufs[s]), cim_ref[rows, cols],
                                    preferred_element_type=F32)
            ys.append(acc)
        y = jnp.concatenate(ys, axis=1) + d_ref[...] * u_prev(s)
        y = jax.nn.gelu(y)
        gate = jnp.dot(y.astype(BF16), gw_ref[...], preferred_element_type=F32) + gbias_ref[...]
        ob_ref[s * sub:(s + 1) * sub, :] = (y * jax.nn.sigmoid(gate)).astype(ob_ref.dtype)

    opens_sequence = (i + tiles_per_seq - 1) % tiles_per_seq == 0
    carry = [(jnp.where(opens_sequence, 0.0, cre_st[idx]), jnp.where(opens_sequence, 0.0, cim_st[idx]))
             for idx in range(n_slab)]
    xb = x_ref[...].astype(BF16)
    w = SB_WIDTH
    g0 = 3 * w + width
    half = d_model // 2

    def project(c0, n):
        return jnp.dot(xb, w_ref[:, c0:c0 + n], preferred_element_type=F32)

    def gate(out_ref, c0, lo):
        out_ref[:, lo:lo + half] = jax.nn.sigmoid(project(c0 + lo, half)).astype(BF16)

    q_ref[...] = (project(0, w) * q_scale).astype(BF16)
    drive(0)
    carry = scan(0, carry)
    k_ref[...] = project(w, w).astype(BF16)
    v_ref[...] = project(2 * w, w).astype(BF16)
    gate(ga_ref, g0, 0)
    gate(ga_ref, g0, half)
    gate(gb_ref, g0 + d_model, 0)
    readout(0)
    gate(gb_ref, g0 + d_model, half)
    for idx in range(n_slab):
        cre_st[idx] = carry[idx][0]
        cim_st[idx] = carry[idx][1]
    u_buf[slot] = project(3 * w, width)


def _block_diag(blocks, dtype):
    g, r, c = blocks.shape
    wide = blocks.transpose(1, 0, 2).reshape(r, g * c)
    row_g = lax.broadcasted_iota(jnp.int32, (g * r, g * c), 0) // r
    col_g = lax.broadcasted_iota(jnp.int32, (g * r, g * c), 1) // c
    return jnp.where(row_g == col_g, jnp.tile(wide, (g, 1)), 0.0).astype(dtype)


def _in_proj_s5(x2, seq, w_in_bf, lam_re, lam_im, log_dt, b_re, b_im, c_re, c_im, d_skip, glu_w, glu_b):
    t, d = x2.shape
    width = d_skip.shape[0]
    lb_re, lb_im, bb_re, bb_im = _s5_discretize(lam_re, lam_im, log_dt, b_re, b_im)
    n_slab = lb_re.size // (SUBLANES * LANES)
    bmat_re = _block_diag(bb_re.transpose(0, 2, 1), BF16)
    bmat_im = _block_diag(bb_im.transpose(0, 2, 1), BF16)
    cmat_re = _block_diag(c_re.transpose(0, 2, 1), BF16)
    cmat_im = _block_diag(c_im.transpose(0, 2, 1), BF16)
    a_re = lb_re.reshape(n_slab, SUBLANES, LANES)
    a_im = lb_im.reshape(n_slab, SUBLANES, LANES)
    tm = ROW_TILE
    n_tiles = t // tm
    cur = lambda n: pl.BlockSpec((tm, n), lambda i: (jnp.minimum(i, n_tiles - 1), 0))
    prev = lambda n: pl.BlockSpec((tm, n), lambda i: (jnp.maximum(i - 1, 0), 0))
    sds = lambda n, dt: jax.ShapeDtypeStruct((t, n), dt)
    consts = (w_in_bf, bmat_re, bmat_im, cmat_re, cmat_im, a_re, a_im,
              d_skip.reshape(1, width), glu_w.astype(BF16), glu_b.reshape(1, width))
    slab_buf = pltpu.VMEM((n_slab, S5_SUB * SUBLANES, LANES), F32)
    carry = pltpu.VMEM((n_slab, SUBLANES, LANES), F32)
    return pl.pallas_call(
        functools.partial(_in_proj_s5_kernel, q_scale=LOG2_E * SB_HEAD_DIM ** -0.5,
                          tiles_per_seq=seq // tm),
        grid=(n_tiles + 1,),
        in_specs=[cur(d)] + [_const_spec(c.shape) for c in consts],
        out_specs=[cur(SB_WIDTH), cur(SB_WIDTH), cur(SB_WIDTH), cur(d), cur(d), prev(width)],
        out_shape=[sds(SB_WIDTH, BF16), sds(SB_WIDTH, BF16), sds(SB_WIDTH, BF16),
                   sds(d, BF16), sds(d, BF16), sds(width, BF16)],
        scratch_shapes=[pltpu.VMEM((2, tm, width), F32), carry, carry]
                       + [slab_buf] * (2 * (tm // S5_SUB)),
        compiler_params=pltpu.CompilerParams(dimension_semantics=("arbitrary",),
                                             vmem_limit_bytes=VMEM_LIMIT),
        name="in_proj_s5",
    )(x2, *consts)


def _suffix_sum_matrix():
    j = lax.broadcasted_iota(jnp.int32, (SUFFIX_UNIT, SUFFIX_UNIT), 0)
    s = lax.broadcasted_iota(jnp.int32, (SUFFIX_UNIT, SUFFIX_UNIT), 1)
    return jnp.where(j > s, 1.0, 0.0).astype(BF16)


def _attn_kernel(q_ref, k_ref, v_ref, m_ref, vis_ref, o_ref):
    tq = q_ref.shape[0]
    blk = SUFFIX_UNIT
    n_blk = tq // blk
    i = pl.program_id(2)
    lane = lax.broadcasted_iota(jnp.int32, (1, LANES), 1)
    q2 = q_ref[...]
    zero = jnp.zeros_like(q2)
    heads = (jnp.where(lane < SB_HEAD_DIM, q2, zero), jnp.where(lane >= SB_HEAD_DIM, q2, zero))
    qs = jnp.concatenate([h[b * blk:(b + 1) * blk] for b in range(n_blk) for h in heads], axis=0)
    m = m_ref[...]

    def unit(start, acc, ls, r0, mask=None, gain=None):
        def masked(x):
            if gain is not None:
                x = x * gain
            if mask is None:
                return x
            top = x[:mask.shape[0]] * mask
            return top if mask.shape[0] == x.shape[0] else jnp.concatenate([top, x[mask.shape[0]:]], axis=0)

        ks = k_ref[pl.ds(start, blk), :]
        vs = v_ref[pl.ds(start, blk), :]
        z = lax.dot_general(qs[r0:], ks, (((1,), (1,)), ((), ())), preferred_element_type=F32)
        sp = jnp.maximum(z, jnp.log(1.0 + jnp.exp2(jnp.minimum(z, SP_LINEAR_FROM))) * LOG2_E)
        d = z - sp
        hi = masked(sp).astype(BF16)
        suffix = jnp.dot(hi, m, preferred_element_type=F32)
        ls_in = ls[r0:]
        w = masked(jnp.exp2(d - suffix + jnp.concatenate([ls_in, ls_in], axis=1)))
        acc_out = acc[r0:] + jnp.dot(w.astype(BF16), vs, preferred_element_type=F32)
        ls_out = ls_in - (suffix[:, :1] + hi[:, :1].astype(F32))
        if r0 == 0:
            return acc_out, ls_out
        return (jnp.concatenate([acc[:r0], acc_out], axis=0),
                jnp.concatenate([ls[:r0], ls_out], axis=0))

    def diagonal_units(acc, ls):
        for j in reversed(range(n_blk)):
            acc, ls = unit(pl.multiple_of(i * tq + j * blk, blk), acc, ls, 2 * j * blk,
                           mask=vis_ref[...])
        return acc, ls

    zeros = jnp.zeros((2 * tq, LANES), F32)
    n_older = i * n_blk

    acc, ls = diagonal_units(zeros, zeros)
    prev_start = pl.multiple_of(jnp.maximum(i * tq - blk, 0), blk)
    acc, ls = unit(prev_start, acc, ls, 0, gain=(i > 0).astype(F32))

    def more(c):
        return jnp.logical_and(c[0] < n_older, jnp.max(c[2]) > LS_FLOOR)

    def older_unit(c):
        n, acc, ls = c
        acc, ls = unit(pl.multiple_of((n_older - 1 - n) * blk, blk), acc, ls, 0)
        return n + 1, acc, ls

    _, acc, ls = lax.while_loop(more, older_unit, (jnp.minimum(i, 1), acc, ls))
    per_head = [jnp.concatenate([acc[(2 * b + h) * blk:(2 * b + h + 1) * blk] for b in range(n_blk)],
                                axis=0) for h in range(2)]
    o_ref[...] = jnp.where(lane < SB_HEAD_DIM, per_head[0], per_head[1]).astype(o_ref.dtype)


def _causal_visibility():
    r = lax.broadcasted_iota(jnp.int32, (2 * SUFFIX_UNIT, SUFFIX_UNIT), 0)
    col = lax.broadcasted_iota(jnp.int32, (2 * SUFFIX_UNIT, SUFFIX_UNIT), 1)
    return (col < r % SUFFIX_UNIT).astype(F32)


def _attention(q, k, v, bsz, seq):
    t = q.shape[0]
    tq = ATTN_TQ
    nq = seq // tq
    n_blk = SB_WIDTH // LANES
    qspec = pl.BlockSpec((tq, LANES), lambda b, h, i: (b * nq + i, h))
    kvspec = pl.BlockSpec((seq, LANES), lambda b, h, i: (b, h))
    return pl.pallas_call(
        _attn_kernel,
        grid=(bsz, n_blk, nq),
        in_specs=[qspec, kvspec, kvspec, _const_spec((SUFFIX_UNIT, SUFFIX_UNIT)),
                  _const_spec((2 * SUFFIX_UNIT, SUFFIX_UNIT))],
        out_specs=qspec,
        out_shape=jax.ShapeDtypeStruct((t, SB_WIDTH), BF16),
        compiler_params=pltpu.CompilerParams(
            dimension_semantics=("parallel", "parallel", "arbitrary"),
            vmem_limit_bytes=VMEM_LIMIT),
        name="stickbreak_attention",
    )(q, k, v, _suffix_sum_matrix(), _causal_visibility())


def _merge_ln1_kernel(x_ref, oa_ref, ob_ref, ga_ref, gb_ref, wa_ref, wb_ref, wo_ref,
                      g_ref, b_ref, h_ref, *, alpha):
    sub = x_ref.shape[0] // 2
    for r0 in (0, sub):
        rows = slice(r0, r0 + sub)
        pa = jnp.dot(oa_ref[rows, :], wa_ref[...], preferred_element_type=F32)
        pb = jnp.dot(ob_ref[rows, :], wb_ref[...], preferred_element_type=F32)
        merged = ga_ref[rows, :].astype(F32) * pa + gb_ref[rows, :].astype(F32) * pb
        r = alpha * x_ref[rows, :] + jnp.dot(merged.astype(BF16), wo_ref[...],
                                             preferred_element_type=F32)
        h_ref[rows, :] = _layer_norm(r, g_ref[...], b_ref[...])


def _merge_ln1(x2, o_a, o_b, sg_a, sg_b, w_a, w_b, w_out, ln_g, ln_b, alpha):
    t, d = x2.shape
    tm = ROW_TILE
    row = lambda n: pl.BlockSpec((tm, n), lambda i: (i, 0))
    return pl.pallas_call(
        functools.partial(_merge_ln1_kernel, alpha=alpha),
        grid=(t // tm,),
        in_specs=[row(d), row(o_a.shape[1]), row(o_b.shape[1]), row(d), row(d),
                  _const_spec(w_a.shape), _const_spec(w_b.shape), _const_spec(w_out.shape),
                  _const_spec((1, d)), _const_spec((1, d))],
        out_specs=row(d),
        out_shape=jax.ShapeDtypeStruct((t, d), F32),
        compiler_params=pltpu.CompilerParams(dimension_semantics=("parallel",),
                                             vmem_limit_bytes=VMEM_LIMIT),
        name="merge_ln1",
    )(x2, o_a, o_b, sg_a, sg_b, w_a, w_b, w_out, ln_g.reshape(1, d), ln_b.reshape(1, d))


def _ffn_ple_ln2_kernel(h_ref, p_ref, wg_ref, wu_ref, wd_ref, wpg_ref, wpp_ref,
                        g_ref, b_ref, o_ref, *, alpha, ff_chunk):
    h = h_ref[...]
    hb = h.astype(BF16)
    d_ff = wg_ref.shape[1]
    ffn = jnp.zeros(h.shape, F32)
    for c0 in range(0, d_ff, ff_chunk):
        gate = jnp.dot(hb, wg_ref[:, c0:c0 + ff_chunk], preferred_element_type=F32)
        up = jnp.dot(hb, wu_ref[:, c0:c0 + ff_chunk], preferred_element_type=F32)
        act = (jax.nn.silu(gate) * up).astype(BF16)
        ffn = ffn + jnp.dot(act, wd_ref[c0:c0 + ff_chunk, :], preferred_element_type=F32)
    ple_gate = jax.nn.sigmoid(jnp.dot(hb, wpg_ref[...], preferred_element_type=F32))
    ple = ple_gate * jnp.dot(p_ref[...].astype(BF16), wpp_ref[...], preferred_element_type=F32)
    o_ref[...] = _layer_norm(alpha * h + ffn + ple, g_ref[...], b_ref[...])


def _ffn_ple_ln2(h1, p2, w_gate, w_up, w_down, w_pg, w_pp, ln_g, ln_b, alpha):
    t, d = h1.shape
    tm = ROW_TILE
    d_ff = w_gate.shape[1]
    ff_chunk = 2 * LANES if d_ff % (2 * LANES) == 0 else d_ff
    row = lambda n: pl.BlockSpec((tm, n), lambda i: (i, 0))
    return pl.pallas_call(
        functools.partial(_ffn_ple_ln2_kernel, alpha=alpha, ff_chunk=ff_chunk),
        grid=(t // tm,),
        in_specs=[row(d), row(p2.shape[1]),
                  _const_spec(w_gate.shape), _const_spec(w_up.shape), _const_spec(w_down.shape),
                  _const_spec(w_pg.shape), _const_spec(w_pp.shape),
                  _const_spec((1, d)), _const_spec((1, d))],
        out_specs=row(d),
        out_shape=jax.ShapeDtypeStruct((t, d), F32),
        compiler_params=pltpu.CompilerParams(dimension_semantics=("parallel",),
                                             vmem_limit_bytes=VMEM_LIMIT),
        name="ffn_ple_ln2",
    )(h1, p2, w_gate, w_up, w_down, w_pg, w_pp, ln_g.reshape(1, d), ln_b.reshape(1, d))


def kernel(x, p, w_in, ssm_lambda_re, ssm_lambda_im, ssm_log_dt, ssm_b_re, ssm_b_im, ssm_c_re, ssm_c_im, ssm_d, ssm_glu_w, ssm_glu_b, w_branch_a, w_branch_b, w_out, ln1_g, ln1_b, ffn_w_gate, ffn_w_up, ffn_w_down, ple_w_gate, ple_w_proj, ln2_g, ln2_b):
    bsz, seq, d = x.shape
    depth = w_in.shape[0]
    alpha = (2 * depth) ** 0.25
    t = bsz * seq
    h = x.reshape(t, d)
    for i in range(depth):
        q, k, v, sg_a, sg_b, o_b = _in_proj_s5(
            h, seq, w_in[i].astype(BF16), ssm_lambda_re[i], ssm_lambda_im[i], ssm_log_dt[i],
            ssm_b_re[i], ssm_b_im[i], ssm_c_re[i], ssm_c_im[i], ssm_d[i], ssm_glu_w[i], ssm_glu_b[i])
        o_a = _attention(q, k, v, bsz, seq)
        h = _merge_ln1(h, o_a, o_b, sg_a, sg_b, w_branch_a[i].astype(BF16), w_branch_b[i].astype(BF16),
                       w_out[i].astype(BF16), ln1_g[i], ln1_b[i], alpha)
        h = _ffn_ple_ln2(h, p[i].reshape(t, -1), ffn_w_gate[i].astype(BF16), ffn_w_up[i].astype(BF16),
                         ffn_w_down[i].astype(BF16), ple_w_gate[i].astype(BF16),
                         ple_w_proj[i].astype(BF16), ln2_g[i], ln2_b[i], alpha)
    return h.reshape(bsz, seq, d)
```

```python
import functools

import jax
import jax.numpy as jnp
from jax import lax
from jax.experimental import pallas as pl
from jax.experimental.pallas import tpu as pltpu

F32 = jnp.float32
BF16 = jnp.bfloat16

LN_EPS = 1e-5
SB_HEADS = 8
SB_HEAD_DIM = 64
SB_WIDTH = SB_HEADS * SB_HEAD_DIM
SSM_GROUP = 16
SSM_STATE = 64

LANES = 128
SUBLANES = 8
SUFFIX_UNIT = 2 * LANES
LOG2_E = 1.4426950408889634
SP_LINEAR_FROM = 126.0
LS_FLOOR = -200.0
VMEM_LIMIT = 56 * 1024 * 1024

ROW_TILE = 512
ATTN_TQ = 512


def _const_spec(shape):
    nd = len(shape)
    return pl.BlockSpec(shape, lambda *_: (0,) * nd, pipeline_mode=pl.Buffered(1))


def _layer_norm(r, g, b):
    mu = jnp.mean(r, axis=-1, keepdims=True)
    d = r - mu
    var = jnp.mean(d * d, axis=-1, keepdims=True)
    return d * lax.rsqrt(var + LN_EPS) * g + b


def _s5_discretize_kernel(lre_ref, lim_ref, ldt_ref, bre_ref, bim_ref, cre_ref, cim_ref,
                          a2re_ref, a2im_ref, bbre_ref, bbim_ref, abre_ref, abim_ref,
                          acre_ref, acim_ref, a2cre_ref, a2cim_ref):
    lam_re = lre_ref[...]
    lam_im = lim_ref[...]
    dt = jnp.exp(ldt_ref[...])
    mag = jnp.exp(lam_re * dt)
    a_re = mag * jnp.cos(lam_im * dt)
    a_im = mag * jnp.sin(lam_im * dt)
    den = lam_re * lam_re + lam_im * lam_im
    n_re = a_re - 1.0
    coef_re = (n_re * lam_re + a_im * lam_im) / den
    coef_im = (a_im * lam_re - n_re * lam_im) / den
    b_re = bre_ref[...]
    b_im = bim_ref[...]
    bb_re = coef_re * b_re - coef_im * b_im
    bb_im = coef_re * b_im + coef_im * b_re
    a2_re = a_re * a_re - a_im * a_im
    a2_im = 2.0 * a_re * a_im
    c_re = cre_ref[...]
    c_im = cim_ref[...]
    a2re_ref[...] = a2_re
    a2im_ref[...] = a2_im
    bbre_ref[...] = bb_re
    bbim_ref[...] = bb_im
    abre_ref[...] = a_re * bb_re - a_im * bb_im
    abim_ref[...] = a_re * bb_im + a_im * bb_re
    acre_ref[...] = a_re * c_re - a_im * c_im
    acim_ref[...] = a_re * c_im + a_im * c_re
    a2cre_ref[...] = a2_re * c_re - a2_im * c_im
    a2cim_ref[...] = a2_re * c_im + a2_im * c_re


def _s5_pair_terms_kernel(ccre_ref, ccim_ref, bbre_ref, bbim_ref, abre_ref, abim_ref, sel_ref,
                          cb_ref, cab_ref):
    sel = sel_ref[...]
    for c_out in range(ccre_ref.shape[0]):
        c_re = ccre_ref[c_out]
        c_im = ccim_ref[c_out]
        cb_ref[c_out] = jnp.dot(c_re * bbre_ref[...] - c_im * bbim_ref[...], sel,
                                preferred_element_type=F32)
        cab_ref[c_out] = jnp.dot(c_re * abre_ref[...] - c_im * abim_ref[...], sel,
                                 preferred_element_type=F32)


def _s5_constants(lam_re, lam_im, log_dt, b_re, b_im, c_re, c_im):
    g, p = lam_re.shape
    c = b_re.shape[-1]
    rep = lambda a: jnp.repeat(a, c, axis=1)
    flat = lambda a: a.reshape(g, p * c)
    ldt = jnp.broadcast_to(log_dt[:, None], (g, p * c))
    ct_re, ct_im = c_re.transpose(0, 2, 1), c_im.transpose(0, 2, 1)
    shp = jax.ShapeDtypeStruct((g, p * c), F32)
    a2_re, a2_im, bb_re, bb_im, ab_re, ab_im, ac_re, ac_im, a2c_re, a2c_im = pl.pallas_call(
        _s5_discretize_kernel, out_shape=(shp,) * 10, name="s5_discretize",
    )(rep(lam_re), rep(lam_im), ldt, flat(b_re), flat(b_im), flat(ct_re), flat(ct_im))
    cc = lambda m: jnp.repeat(m.transpose(1, 0, 2), c, axis=2)
    sel = (lax.broadcasted_iota(jnp.int32, (p * c, c), 0) % c
           == lax.broadcasted_iota(jnp.int32, (p * c, c), 1)).astype(F32)
    pair = jax.ShapeDtypeStruct((c, g, c), F32)
    cb, cab = pl.pallas_call(
        _s5_pair_terms_kernel, out_shape=(pair, pair), name="s5_pair_terms",
    )(cc(c_re), cc(c_im), bb_re, bb_im, ab_re, ab_im, sel)
    unflat = lambda a: a.reshape(g, p, c)
    return dict(a2_re=a2_re[:, ::c], a2_im=a2_im[:, ::c],
                bb_re=unflat(bb_re), bb_im=unflat(bb_im), ab_re=unflat(ab_re), ab_im=unflat(ab_im),
                ac_re=unflat(ac_re), ac_im=unflat(ac_im), a2c_re=unflat(a2c_re), a2c_im=unflat(a2c_im),
                cb=cb.transpose(1, 0, 2), cab=cab.transpose(1, 0, 2))


def _in_proj_s5_kernel(x_ref, w_ref, wre_ref, wim_ref, rre_ref, rim_ref, t_ref, are_ref, aim_ref,
                       d_ref, gw_ref, gbias_ref,
                       q_ref, k_ref, v_ref, ga_ref, gb_ref, ob_ref,
                       u_buf, y_buf, cre_st, cim_st, sre_buf, sim_buf, *, q_scale, tiles_per_seq):
    i = pl.program_id(0)
    tm, d_model = x_ref.shape
    n_ublk = u_buf.shape[1]
    width = n_ublk * LANES
    n_pair = tm // 2
    n_chunk = wre_ref.shape[2] * n_ublk // LANES
    n_slab = n_chunk // SUBLANES
    ch_per_ublk = n_chunk // n_ublk

    @pl.when(i == 0)
    def _():
        u_buf[...] = jnp.zeros_like(u_buf)
        cre_st[...] = jnp.zeros_like(cre_st)
        cim_st[...] = jnp.zeros_like(cim_st)

    slot = i % 2

    def u_pairs(blk):
        even = u_buf[1 - slot, blk, pl.ds(0, n_pair, stride=2), :]
        odd = u_buf[1 - slot, blk, pl.ds(1, n_pair, stride=2), :]
        return jnp.concatenate([even, odd], axis=1).astype(BF16)

    def slab_rows(c):
        return pl.ds(c % SUBLANES, n_pair, stride=SUBLANES)

    def drive():
        for blk in range(n_ublk):
            up = u_pairs(blk)
            for w2_ref, buf in ((wre_ref, sre_buf), (wim_ref, sim_buf)):
                val = jnp.dot(up, w2_ref[blk], preferred_element_type=F32)
                for e in range(ch_per_ublk):
                    c = blk * ch_per_ublk + e
                    buf[c // SUBLANES, slab_rows(c), :] = val[:, e * LANES:(e + 1) * LANES]

    a_re = [are_ref[k] for k in range(n_slab)]
    a_im = [aim_ref[k] for k in range(n_slab)]

    def scan(carry):
        for t in range(n_pair):
            tok = pl.ds(t * SUBLANES, SUBLANES)
            new = []
            for idx in range(n_slab):
                s_re, s_im = carry[idx]
                n_re = a_re[idx] * s_re - a_im[idx] * s_im + sre_buf[idx, tok, :]
                n_im = a_re[idx] * s_im + a_im[idx] * s_re + sim_buf[idx, tok, :]
                sre_buf[idx, tok, :] = s_re
                sim_buf[idx, tok, :] = s_im
                new.append((n_re, n_im))
            carry = new
        return carry

    def readout():
        for blk in range(n_ublk):
            acc = jnp.dot(u_pairs(blk), t_ref[blk], preferred_element_type=F32)
            for half in range(ch_per_ublk // 2):
                c0 = blk * ch_per_ublk + 2 * half
                rows = slice(c0 * LANES, (c0 + 2) * LANES)
                cols = slice(blk * 2 * LANES, (blk + 1) * 2 * LANES)

                def chunk_pair(buf):
                    parts = [buf[(c0 + e) // SUBLANES, slab_rows(c0 + e), :] for e in range(2)]
                    return jnp.concatenate(parts, axis=1).astype(BF16)

                acc = acc + jnp.dot(chunk_pair(sre_buf), rre_ref[rows, cols],
                                    preferred_element_type=F32)
                acc = acc + jnp.dot(chunk_pair(sim_buf), rim_ref[rows, cols],
                                    preferred_element_type=F32)
            y_buf[blk, pl.ds(0, n_pair, stride=2), :] = acc[:, :LANES]
            y_buf[blk, pl.ds(1, n_pair, stride=2), :] = acc[:, LANES:]
        y = jnp.concatenate([y_buf[blk] + d_ref[:, blk * LANES:(blk + 1) * LANES] * u_buf[1 - slot, blk]
                             for blk in range(n_ublk)], axis=1)
        y = jax.nn.gelu(y)
        gate = jnp.dot(y.astype(BF16), gw_ref[...], preferred_element_type=F32) + gbias_ref[...]
        ob_ref[...] = (y * jax.nn.sigmoid(gate)).astype(ob_ref.dtype)

    opens_sequence = (i + tiles_per_seq - 1) % tiles_per_seq == 0
    carry = [(jnp.where(opens_sequence, 0.0, cre_st[idx]), jnp.where(opens_sequence, 0.0, cim_st[idx]))
             for idx in range(n_slab)]

    xb = x_ref[...].astype(BF16)
    w = SB_WIDTH
    g0 = 3 * w + width
    half = d_model // 2

    def project(c0, n):
        return jnp.dot(xb, w_ref[:, c0:c0 + n], preferred_element_type=F32)

    def gate(out_ref, c0, lo):
        out_ref[:, lo:lo + half] = jax.nn.sigmoid(project(c0 + lo, half)).astype(BF16)

    q_ref[...] = (project(0, w) * q_scale).astype(BF16)
    drive()
    carry = scan(carry)
    k_ref[...] = project(w, w).astype(BF16)
    v_ref[...] = project(2 * w, w).astype(BF16)
    gate(ga_ref, g0, 0)
    readout()
    gate(ga_ref, g0, half)
    gate(gb_ref, g0 + d_model, 0)
    gate(gb_ref, g0 + d_model, half)
    for idx in range(n_slab):
        cre_st[idx] = carry[idx][0]
        cim_st[idx] = carry[idx][1]
    u_new = project(3 * w, width)
    for blk in range(n_ublk):
        u_buf[slot, blk] = u_new[:, blk * LANES:(blk + 1) * LANES]


def _block_diag(blocks, dtype):
    g, r, c = blocks.shape
    wide = blocks.transpose(1, 0, 2).reshape(r, g * c)
    row_g = lax.broadcasted_iota(jnp.int32, (g * r, g * c), 0) // r
    col_g = lax.broadcasted_iota(jnp.int32, (g * r, g * c), 1) // c
    return jnp.where(row_g == col_g, jnp.tile(wide, (g, 1)), 0.0).astype(dtype)


def _in_proj_s5(x2, seq, w_in_bf, lam_re, lam_im, log_dt, b_re, b_im, c_re, c_im, d_skip, glu_w, glu_b):
    t, d = x2.shape
    width = d_skip.shape[0]
    n_ublk = width // LANES
    k = _s5_constants(lam_re, lam_im, log_dt, b_re, b_im, c_re, c_im)
    n_state = k["a2_re"].size
    n_slab = n_state // (SUBLANES * LANES)
    per_blk = n_state // n_ublk

    def drive_matrix(ab, bb):
        even = _block_diag(ab.transpose(0, 2, 1), BF16)
        odd = _block_diag(bb.transpose(0, 2, 1), BF16)
        return jnp.stack([jnp.concatenate([m[b * LANES:(b + 1) * LANES, b * per_blk:(b + 1) * per_blk]
                                           for m in (even, odd)], axis=0) for b in range(n_ublk)])

    def readout_matrix(ac, a2c, sign):
        even = _block_diag(sign * ac, F32).reshape(n_state, n_ublk, LANES)
        odd = _block_diag(sign * a2c, F32).reshape(n_state, n_ublk, LANES)
        return jnp.stack([even, odd], axis=2).reshape(n_state, 2 * width).astype(BF16)

    cb = _block_diag(k["cb"].transpose(0, 2, 1), F32)
    cab = _block_diag(k["cab"].transpose(0, 2, 1), F32)
    pair_matrix = jnp.stack([
        jnp.concatenate([
            jnp.concatenate([cb[s_, s_], cab[s_, s_]], axis=1),
            jnp.concatenate([jnp.zeros((LANES, LANES), F32), cb[s_, s_]], axis=1)], axis=0)
        for s_ in (slice(b * LANES, (b + 1) * LANES) for b in range(n_ublk))]).astype(BF16)
    a_re = k["a2_re"].reshape(n_slab, SUBLANES, LANES)
    a_im = k["a2_im"].reshape(n_slab, SUBLANES, LANES)
    tm = ROW_TILE
    n_tiles = t // tm
    cur = lambda n: pl.BlockSpec((tm, n), lambda i: (jnp.minimum(i, n_tiles - 1), 0))
    prev = lambda n: pl.BlockSpec((tm, n), lambda i: (jnp.maximum(i - 1, 0), 0))
    sds = lambda n, dt: jax.ShapeDtypeStruct((t, n), dt)
    consts = (w_in_bf, drive_matrix(k["ab_re"], k["bb_re"]), drive_matrix(k["ab_im"], k["bb_im"]),
              readout_matrix(k["ac_re"], k["a2c_re"], 1.0), readout_matrix(k["ac_im"], k["a2c_im"], -1.0),
              pair_matrix, a_re, a_im,
              d_skip.reshape(1, width), glu_w.astype(BF16), glu_b.reshape(1, width))
    slab_buf = pltpu.VMEM((n_slab, (tm // 2) * SUBLANES, LANES), F32)
    carry = pltpu.VMEM((n_slab, SUBLANES, LANES), F32)
    return pl.pallas_call(
        functools.partial(_in_proj_s5_kernel, q_scale=LOG2_E * SB_HEAD_DIM ** -0.5,
                          tiles_per_seq=seq // tm),
        grid=(n_tiles + 1,),
        in_specs=[cur(d)] + [_const_spec(c.shape) for c in consts],
        out_specs=[cur(SB_WIDTH), cur(SB_WIDTH), cur(SB_WIDTH), cur(d), cur(d), prev(width)],
        out_shape=[sds(SB_WIDTH, BF16), sds(SB_WIDTH, BF16), sds(SB_WIDTH, BF16),
                   sds(d, BF16), sds(d, BF16), sds(width, BF16)],
        scratch_shapes=[pltpu.VMEM((2, n_ublk, tm, LANES), F32), pltpu.VMEM((n_ublk, tm, LANES), F32),
                        carry, carry, slab_buf, slab_buf],
        compiler_params=pltpu.CompilerParams(dimension_semantics=("arbitrary",),
                                             vmem_limit_bytes=VMEM_LIMIT),
        name="in_proj_s5",
    )(x2, *consts)


def _suffix_sum_matrix():
    j = lax.broadcasted_iota(jnp.int32, (SUFFIX_UNIT, SUFFIX_UNIT), 0)
    s = lax.broadcasted_iota(jnp.int32, (SUFFIX_UNIT, SUFFIX_UNIT), 1)
    return jnp.where(j > s, 1.0, 0.0).astype(BF16)


def _attn_kernel(q_ref, k_ref, v_ref, m_ref, vis_ref, o_ref):
    tq = q_ref.shape[0]
    blk = SUFFIX_UNIT
    n_blk = tq // blk
    i = pl.program_id(2)
    lane = lax.broadcasted_iota(jnp.int32, (1, LANES), 1)
    q2 = q_ref[...]
    zero = jnp.zeros_like(q2)
    heads = (jnp.where(lane < SB_HEAD_DIM, q2, zero), jnp.where(lane >= SB_HEAD_DIM, q2, zero))
    qs = jnp.concatenate([h[b * blk:(b + 1) * blk] for b in range(n_blk) for h in heads], axis=0)
    m = m_ref[...]

    def unit(start, acc, ls, r0, mask=None, gain=None):
        def masked(x):
            if gain is not None:
                x = x * gain
            if mask is None:
                return x
            top = x[:mask.shape[0]] * mask
            return top if mask.shape[0] == x.shape[0] else jnp.concatenate([top, x[mask.shape[0]:]], axis=0)

        ks = k_ref[pl.ds(start, blk), :]
        vs = v_ref[pl.ds(start, blk), :]
        z = lax.dot_general(qs[r0:], ks, (((1,), (1,)), ((), ())), preferred_element_type=F32)
        sp = jnp.maximum(z, jnp.log(1.0 + jnp.exp2(jnp.minimum(z, SP_LINEAR_FROM))) * LOG2_E)
        d = z - sp
        hi = masked(sp).astype(BF16)
        suffix = jnp.dot(hi, m, preferred_element_type=F32)
        ls_in = ls[r0:]
        w = masked(jnp.exp2(d - suffix + jnp.concatenate([ls_in, ls_in], axis=1)))
        acc_out = acc[r0:] + jnp.dot(w.astype(BF16), vs, preferred_element_type=F32)
        ls_out = ls_in - (suffix[:, :1] + hi[:, :1].astype(F32))
        if r0 == 0:
            return acc_out, ls_out
        return (jnp.concatenate([acc[:r0], acc_out], axis=0),
                jnp.concatenate([ls[:r0], ls_out], axis=0))

    def diagonal_units(acc, ls):
        for j in reversed(range(n_blk)):
            acc, ls = unit(pl.multiple_of(i * tq + j * blk, blk), acc, ls, 2 * j * blk,
                           mask=vis_ref[...])
        return acc, ls

    zeros = jnp.zeros((2 * tq, LANES), F32)
    n_older = i * n_blk

    acc, ls = diagonal_units(zeros, zeros)
    prev_start = pl.multiple_of(jnp.maximum(i * tq - blk, 0), blk)
    acc, ls = unit(prev_start, acc, ls, 0, gain=(i > 0).astype(F32))

    def more(c):
        return jnp.logical_and(c[0] < n_older, jnp.max(c[2]) > LS_FLOOR)

    def older_unit(c):
        n, acc, ls = c
        acc, ls = unit(pl.multiple_of((n_older - 1 - n) * blk, blk), acc, ls, 0)
        return n + 1, acc, ls

    _, acc, ls = lax.while_loop(more, older_unit, (jnp.minimum(i, 1), acc, ls))
    per_head = [jnp.concatenate([acc[(2 * b + h) * blk:(2 * b + h + 1) * blk] for b in range(n_blk)],
                                axis=0) for h in range(2)]
    o_ref[...] = jnp.where(lane < SB_HEAD_DIM, per_head[0], per_head[1]).astype(o_ref.dtype)


def _causal_visibility():
    r = lax.broadcasted_iota(jnp.int32, (2 * SUFFIX_UNIT, SUFFIX_UNIT), 0)
    col = lax.broadcasted_iota(jnp.int32, (2 * SUFFIX_UNIT, SUFFIX_UNIT), 1)
    return (col < r % SUFFIX_UNIT).astype(F32)


def _attention(q, k, v, bsz, seq):
    t = q.shape[0]
    tq = ATTN_TQ
    nq = seq // tq
    n_blk = SB_WIDTH // LANES
    qspec = pl.BlockSpec((tq, LANES), lambda b, h, i: (b * nq + i, h))
    kvspec = pl.BlockSpec((seq, LANES), lambda b, h, i: (b, h))
    return pl.pallas_call(
        _attn_kernel,
        grid=(bsz, n_blk, nq),
        in_specs=[qspec, kvspec, kvspec, _const_spec((SUFFIX_UNIT, SUFFIX_UNIT)),
                  _const_spec((2 * SUFFIX_UNIT, SUFFIX_UNIT))],
        out_specs=qspec,
        out_shape=jax.ShapeDtypeStruct((t, SB_WIDTH), BF16),
        compiler_params=pltpu.CompilerParams(
            dimension_semantics=("parallel", "parallel", "arbitrary"),
            vmem_limit_bytes=VMEM_LIMIT),
        name="stickbreak_attention",
    )(q, k, v, _suffix_sum_matrix(), _causal_visibility())


def _merge_ln1_kernel(x_ref, oa_ref, ob_ref, ga_ref, gb_ref, wa_ref, wb_ref, wo_ref,
                      g_ref, b_ref, h_ref, *, alpha):
    sub = x_ref.shape[0] // 2
    for r0 in (0, sub):
        rows = slice(r0, r0 + sub)
        pa = jnp.dot(oa_ref[rows, :], wa_ref[...], preferred_element_type=F32)
        pb = jnp.dot(ob_ref[rows, :], wb_ref[...], preferred_element_type=F32)
        merged = ga_ref[rows, :].astype(F32) * pa + gb_ref[rows, :].astype(F32) * pb
        r = alpha * x_ref[rows, :] + jnp.dot(merged.astype(BF16), wo_ref[...],
                                             preferred_element_type=F32)
        h_ref[rows, :] = _layer_norm(r, g_ref[...], b_ref[...])


def _merge_ln1(x2, o_a, o_b, sg_a, sg_b, w_a, w_b, w_out, ln_g, ln_b, alpha):
    t, d = x2.shape
    tm = ROW_TILE
    row = lambda n: pl.BlockSpec((tm, n), lambda i: (i, 0))
    return pl.pallas_call(
        functools.partial(_merge_ln1_kernel, alpha=alpha),
        grid=(t // tm,),
        in_specs=[row(d), row(o_a.shape[1]), row(o_b.shape[1]), row(d), row(d),
                  _const_spec(w_a.shape), _const_spec(w_b.shape), _const_spec(w_out.shape),
                  _const_spec((1, d)), _const_spec((1, d))],
        out_specs=row(d),
        out_shape=jax.ShapeDtypeStruct((t, d), F32),
        compiler_params=pltpu.CompilerParams(dimension_semantics=("parallel",),
                                             vmem_limit_bytes=VMEM_LIMIT),
        name="merge_ln1",
    )(x2, o_a, o_b, sg_a, sg_b, w_a, w_b, w_out, ln_g.reshape(1, d), ln_b.reshape(1, d))


def _ffn_ple_ln2_kernel(h_ref, p_ref, wg_ref, wu_ref, wd_ref, wpg_ref, wpp_ref,
                        g_ref, b_ref, o_ref, *, alpha, ff_chunk):
    h = h_ref[...]
    hb = h.astype(BF16)
    d_ff = wg_ref.shape[1]
    ffn = jnp.zeros(h.shape, F32)
    for c0 in range(0, d_ff, ff_chunk):
        gate = jnp.dot(hb, wg_ref[:, c0:c0 + ff_chunk], preferred_element_type=F32)
        up = jnp.dot(hb, wu_ref[:, c0:c0 + ff_chunk], preferred_element_type=F32)
        act = (jax.nn.silu(gate) * up).astype(BF16)
        ffn = ffn + jnp.dot(act, wd_ref[c0:c0 + ff_chunk, :], preferred_element_type=F32)
    ple_gate = jax.nn.sigmoid(jnp.dot(hb, wpg_ref[...], preferred_element_type=F32))
    ple = ple_gate * jnp.dot(p_ref[...].astype(BF16), wpp_ref[...], preferred_element_type=F32)
    o_ref[...] = _layer_norm(alpha * h + ffn + ple, g_ref[...], b_ref[...])


def _ffn_ple_ln2(h1, p2, w_gate, w_up, w_down, w_pg, w_pp, ln_g, ln_b, alpha):
    t, d = h1.shape
    tm = ROW_TILE
    d_ff = w_gate.shape[1]
    ff_chunk = 2 * LANES if d_ff % (2 * LANES) == 0 else d_ff
    row = lambda n: pl.BlockSpec((tm, n), lambda i: (i, 0))
    return pl.pallas_call(
        functools.partial(_ffn_ple_ln2_kernel, alpha=alpha, ff_chunk=ff_chunk),
        grid=(t // tm,),
        in_specs=[row(d), row(p2.shape[1]),
                  _const_spec(w_gate.shape), _const_spec(w_up.shape), _const_spec(w_down.shape),
                  _const_spec(w_pg.shape), _const_spec(w_pp.shape),
                  _const_spec((1, d)), _const_spec((1, d))],
        out_specs=row(d),
        out_shape=jax.ShapeDtypeStruct((t, d), F32),
        compiler_params=pltpu.CompilerParams(dimension_semantics=("parallel",),
                                             vmem_limit_bytes=VMEM_LIMIT),
        name="ffn_ple_ln2",
    )(h1, p2, w_gate, w_up, w_down, w_pg, w_pp, ln_g.reshape(1, d), ln_b.reshape(1, d))


def kernel(x, p, w_in, ssm_lambda_re, ssm_lambda_im, ssm_log_dt, ssm_b_re, ssm_b_im, ssm_c_re, ssm_c_im, ssm_d, ssm_glu_w, ssm_glu_b, w_branch_a, w_branch_b, w_out, ln1_g, ln1_b, ffn_w_gate, ffn_w_up, ffn_w_down, ple_w_gate, ple_w_proj, ln2_g, ln2_b):
    bsz, seq, d = x.shape
    depth = w_in.shape[0]
    alpha = (2 * depth) ** 0.25
    t = bsz * seq
    h = x.reshape(t, d)
    for i in range(depth):
        q, k, v, sg_a, sg_b, o_b = _in_proj_s5(
            h, seq, w_in[i].astype(BF16), ssm_lambda_re[i], ssm_lambda_im[i], ssm_log_dt[i],
            ssm_b_re[i], ssm_b_im[i], ssm_c_re[i], ssm_c_im[i], ssm_d[i], ssm_glu_w[i], ssm_glu_b[i])
        o_a = _attention(q, k, v, bsz, seq)
        h = _merge_ln1(h, o_a, o_b, sg_a, sg_b, w_branch_a[i].astype(BF16), w_branch_b[i].astype(BF16),
                       w_out[i].astype(BF16), ln1_g[i], ln1_b[i], alpha)
        h = _ffn_ple_ln2(h, p[i].reshape(t, -1), ffn_w_gate[i].astype(BF16), ffn_w_up[i].astype(BF16),
                         ffn_w_down[i].astype(BF16), ple_w_gate[i].astype(BF16),
                         ple_w_proj[i].astype(BF16), ln2_g[i], ln2_b[i], alpha)
    return h.reshape(bsz, seq, d)
```

```python
import functools

import jax
import jax.numpy as jnp
from jax import lax
from jax.experimental import pallas as pl
from jax.experimental.pallas import tpu as pltpu

F32 = jnp.float32
BF16 = jnp.bfloat16

LN_EPS = 1e-5
SB_HEADS = 8
SB_HEAD_DIM = 64
SB_WIDTH = SB_HEADS * SB_HEAD_DIM
SSM_GROUP = 16
SSM_STATE = 64

LANES = 128
SUBLANES = 8
SUFFIX_UNIT = 2 * LANES
LOG2_E = 1.4426950408889634
SP_LINEAR_FROM = 126.0
LS_FLOOR = -200.0
VMEM_LIMIT = 56 * 1024 * 1024

ROW_TILE = 512
ATTN_TQ = 512


def _const_spec(shape):
    nd = len(shape)
    return pl.BlockSpec(shape, lambda *_: (0,) * nd, pipeline_mode=pl.Buffered(1))


def _layer_norm(r, g, b):
    mu = jnp.mean(r, axis=-1, keepdims=True)
    d = r - mu
    var = jnp.mean(d * d, axis=-1, keepdims=True)
    return d * lax.rsqrt(var + LN_EPS) * g + b


def _s5_discretize_kernel(lre_ref, lim_ref, ldt_ref, bre_ref, bim_ref, cre_ref, cim_ref,
                          a2re_ref, a2im_ref, bbre_ref, bbim_ref, abre_ref, abim_ref,
                          acre_ref, acim_ref, a2cre_ref, a2cim_ref):
    lam_re = lre_ref[...]
    lam_im = lim_ref[...]
    dt = jnp.exp(ldt_ref[...])
    mag = jnp.exp(lam_re * dt)
    a_re = mag * jnp.cos(lam_im * dt)
    a_im = mag * jnp.sin(lam_im * dt)
    den = lam_re * lam_re + lam_im * lam_im
    n_re = a_re - 1.0
    coef_re = (n_re * lam_re + a_im * lam_im) / den
    coef_im = (a_im * lam_re - n_re * lam_im) / den
    b_re = bre_ref[...]
    b_im = bim_ref[...]
    bb_re = coef_re * b_re - coef_im * b_im
    bb_im = coef_re * b_im + coef_im * b_re
    a2_re = a_re * a_re - a_im * a_im
    a2_im = 2.0 * a_re * a_im
    c_re = cre_ref[...]
    c_im = cim_ref[...]
    a2re_ref[...] = a2_re
    a2im_ref[...] = a2_im
    bbre_ref[...] = bb_re
    bbim_ref[...] = bb_im
    abre_ref[...] = a_re * bb_re - a_im * bb_im
    abim_ref[...] = a_re * bb_im + a_im * bb_re
    acre_ref[...] = a_re * c_re - a_im * c_im
    acim_ref[...] = a_re * c_im + a_im * c_re
    a2cre_ref[...] = a2_re * c_re - a2_im * c_im
    a2cim_ref[...] = a2_re * c_im + a2_im * c_re


def _s5_pair_terms_kernel(ccre_ref, ccim_ref, bbre_ref, bbim_ref, abre_ref, abim_ref, sel_ref,
                          cb_ref, cab_ref):
    sel = sel_ref[...]
    for c_out in range(ccre_ref.shape[0]):
        c_re = ccre_ref[c_out]
        c_im = ccim_ref[c_out]
        cb_ref[c_out] = jnp.dot(c_re * bbre_ref[...] - c_im * bbim_ref[...], sel,
                                preferred_element_type=F32)
        cab_ref[c_out] = jnp.dot(c_re * abre_ref[...] - c_im * abim_ref[...], sel,
                                 preferred_element_type=F32)


def _s5_constants(lam_re, lam_im, log_dt, b_re, b_im, c_re, c_im):
    g, p = lam_re.shape
    c = b_re.shape[-1]
    rep = lambda a: jnp.repeat(a, c, axis=1)
    flat = lambda a: a.reshape(g, p * c)
    ldt = jnp.broadcast_to(log_dt[:, None], (g, p * c))
    ct_re, ct_im = c_re.transpose(0, 2, 1), c_im.transpose(0, 2, 1)
    shp = jax.ShapeDtypeStruct((g, p * c), F32)
    a2_re, a2_im, bb_re, bb_im, ab_re, ab_im, ac_re, ac_im, a2c_re, a2c_im = pl.pallas_call(
        _s5_discretize_kernel, out_shape=(shp,) * 10, name="s5_discretize",
    )(rep(lam_re), rep(lam_im), ldt, flat(b_re), flat(b_im), flat(ct_re), flat(ct_im))
    cc = lambda m: jnp.repeat(m.transpose(1, 0, 2), c, axis=2)
    sel = (lax.broadcasted_iota(jnp.int32, (p * c, c), 0) % c
           == lax.broadcasted_iota(jnp.int32, (p * c, c), 1)).astype(F32)
    pair = jax.ShapeDtypeStruct((c, g, c), F32)
    cb, cab = pl.pallas_call(
        _s5_pair_terms_kernel, out_shape=(pair, pair), name="s5_pair_terms",
    )(cc(c_re), cc(c_im), bb_re, bb_im, ab_re, ab_im, sel)
    unflat = lambda a: a.reshape(g, p, c)
    return dict(a2_re=a2_re[:, ::c], a2_im=a2_im[:, ::c],
                bb_re=unflat(bb_re), bb_im=unflat(bb_im), ab_re=unflat(ab_re), ab_im=unflat(ab_im),
                ac_re=unflat(ac_re), ac_im=unflat(ac_im), a2c_re=unflat(a2c_re), a2c_im=unflat(a2c_im),
                cb=cb.transpose(1, 0, 2), cab=cab.transpose(1, 0, 2))


def _in_proj_s5_kernel(x_ref, w_ref, wre_ref, wim_ref, rre_ref, rim_ref, t_ref, are_ref, aim_ref,
                       d_ref, gw_ref, gbias_ref,
                       q_ref, k_ref, v_ref, ga_ref, gb_ref, ob_ref,
                       u_buf, y_buf, cre_st, cim_st, sre_buf, sim_buf, *, q_scale, tiles_per_seq):
    i = pl.program_id(0)
    tm, d_model = x_ref.shape
    n_ublk = u_buf.shape[1]
    width = n_ublk * LANES
    n_pair = tm // 2
    n_chunk = wre_ref.shape[2] * n_ublk // LANES
    n_slab = n_chunk // SUBLANES
    ch_per_ublk = n_chunk // n_ublk

    @pl.when(i == 0)
    def _():
        u_buf[...] = jnp.zeros_like(u_buf)
        cre_st[...] = jnp.zeros_like(cre_st)
        cim_st[...] = jnp.zeros_like(cim_st)

    slot = i % 2

    def u_pairs(blk):
        even = u_buf[1 - slot, blk, pl.ds(0, n_pair, stride=2), :]
        odd = u_buf[1 - slot, blk, pl.ds(1, n_pair, stride=2), :]
        return jnp.concatenate([even, odd], axis=1).astype(BF16)

    def slab_rows(c):
        return pl.ds(c % SUBLANES, n_pair, stride=SUBLANES)

    def drive():
        for blk in range(n_ublk):
            up = u_pairs(blk)
            for w2_ref, buf in ((wre_ref, sre_buf), (wim_ref, sim_buf)):
                val = jnp.dot(up, w2_ref[blk], preferred_element_type=F32)
                for e in range(ch_per_ublk):
                    c = blk * ch_per_ublk + e
                    buf[c // SUBLANES, slab_rows(c), :] = val[:, e * LANES:(e + 1) * LANES]

    a_re = [are_ref[k] for k in range(n_slab)]
    a_im = [aim_ref[k] for k in range(n_slab)]

    def scan(carry):
        for t in range(n_pair):
            tok = pl.ds(t * SUBLANES, SUBLANES)
            new = []
            for idx in range(n_slab):
                s_re, s_im = carry[idx]
                n_re = a_re[idx] * s_re - a_im[idx] * s_im + sre_buf[idx, tok, :]
                n_im = a_re[idx] * s_im + a_im[idx] * s_re + sim_buf[idx, tok, :]
                sre_buf[idx, tok, :] = s_re
                sim_buf[idx, tok, :] = s_im
                new.append((n_re, n_im))
            carry = new
        return carry

    def readout():
        for blk in range(n_ublk):
            acc = jnp.dot(u_pairs(blk), t_ref[blk], preferred_element_type=F32)
            for half in range(ch_per_ublk // 2):
                c0 = blk * ch_per_ublk + 2 * half
                rows = slice(c0 * LANES, (c0 + 2) * LANES)
                cols = slice(blk * 2 * LANES, (blk + 1) * 2 * LANES)

                def chunk_pair(buf):
                    parts = [buf[(c0 + e) // SUBLANES, slab_rows(c0 + e), :] for e in range(2)]
                    return jnp.concatenate(parts, axis=1).astype(BF16)

                acc = acc + jnp.dot(chunk_pair(sre_buf), rre_ref[rows, cols],
                                    preferred_element_type=F32)
                acc = acc + jnp.dot(chunk_pair(sim_buf), rim_ref[rows, cols],
                                    preferred_element_type=F32)
            y_buf[blk, pl.ds(0, n_pair, stride=2), :] = acc[:, :LANES]
            y_buf[blk, pl.ds(1, n_pair, stride=2), :] = acc[:, LANES:]
        y = jnp.concatenate([y_buf[blk] + d_ref[:, blk * LANES:(blk + 1) * LANES] * u_buf[1 - slot, blk]
                             for blk in range(n_ublk)], axis=1)
        y = jax.nn.gelu(y)
        gate = jnp.dot(y.astype(BF16), gw_ref[...], preferred_element_type=F32) + gbias_ref[...]
        ob_ref[...] = (y * jax.nn.sigmoid(gate)).astype(ob_ref.dtype)

    opens_sequence = (i + tiles_per_seq - 1) % tiles_per_seq == 0
    carry = [(jnp.where(opens_sequence, 0.0, cre_st[idx]), jnp.where(opens_sequence, 0.0, cim_st[idx]))
             for idx in range(n_slab)]

    xb = x_ref[...].astype(BF16)
    w = SB_WIDTH
    g0 = 3 * w + width
    half = d_model // 2

    def project(c0, n):
        return jnp.dot(xb, w_ref[:, c0:c0 + n], preferred_element_type=F32)

    def gate(out_ref, c0, lo):
        out_ref[:, lo:lo + half] = jax.nn.sigmoid(project(c0 + lo, half)).astype(BF16)

    q_ref[...] = (project(0, w) * q_scale).astype(BF16)
    drive()
    carry = scan(carry)
    k_ref[...] = project(w, w).astype(BF16)
    v_ref[...] = project(2 * w, w).astype(BF16)
    gate(ga_ref, g0, 0)
    readout()
    gate(ga_ref, g0, half)
    gate(gb_ref, g0 + d_model, 0)
    gate(gb_ref, g0 + d_model, half)
    for idx in range(n_slab):
        cre_st[idx] = carry[idx][0]
        cim_st[idx] = carry[idx][1]
    u_new = project(3 * w, width)
    for blk in range(n_ublk):
        u_buf[slot, blk] = u_new[:, blk * LANES:(blk + 1) * LANES]


def _in_proj_s5(x2, seq, w_in_bf, lam_re, lam_im, log_dt, b_re, b_im, c_re, c_im, d_skip, glu_w, glu_b):
    t, d = x2.shape
    width = d_skip.shape[0]
    n_ublk = width // LANES
    k = _s5_constants(lam_re, lam_im, log_dt, b_re, b_im, c_re, c_im)
    n_state = k["a2_re"].size
    n_slab = n_state // (SUBLANES * LANES)
    per_blk = n_state // n_ublk

    g, p_, c_ = k["bb_re"].shape
    g_per_blk = g // n_ublk

    def same_group(shape, axis_a, axis_b):
        return lax.broadcasted_iota(jnp.int32, shape, axis_a) == lax.broadcasted_iota(jnp.int32, shape, axis_b)

    def drive_matrix(ab, bb):
        x = jnp.stack([ab, bb]).reshape(2, n_ublk, g_per_blk, p_, c_)
        x = x.transpose(1, 0, 4, 2, 3)[:, :, None]
        shape = (n_ublk, 2, g_per_blk, c_, g_per_blk, p_)
        m = jnp.where(same_group(shape, 2, 4), jnp.broadcast_to(x, shape), 0.0)
        return m.reshape(n_ublk, 2 * LANES, per_blk).astype(BF16)

    def readout_matrix(ac, a2c, sign):
        y = (sign * jnp.stack([ac, a2c], axis=2)).reshape(n_ublk, g_per_blk, p_, 1, 2, 1, c_)
        shape = (n_ublk, g_per_blk, p_, n_ublk, 2, g_per_blk, c_)
        keep = same_group(shape, 0, 3) & same_group(shape, 1, 5)
        return jnp.where(keep, jnp.broadcast_to(y, shape), 0.0).reshape(n_state, 2 * width).astype(BF16)

    zero = jnp.zeros_like(k["cb"])
    t4 = jnp.stack([jnp.stack([k["cb"], k["cab"]]), jnp.stack([zero, k["cb"]])])
    t4 = t4.reshape(2, 2, n_ublk, g_per_blk, c_, c_).transpose(2, 0, 5, 1, 3, 4)[:, :, None]
    shape = (n_ublk, 2, g_per_blk, c_, 2, g_per_blk, c_)
    pair_matrix = jnp.where(same_group(shape, 2, 5), jnp.broadcast_to(t4, shape), 0.0
                            ).reshape(n_ublk, 2 * LANES, 2 * LANES).astype(BF16)
    a_re = k["a2_re"].reshape(n_slab, SUBLANES, LANES)
    a_im = k["a2_im"].reshape(n_slab, SUBLANES, LANES)
    tm = ROW_TILE
    n_tiles = t // tm
    cur = lambda n: pl.BlockSpec((tm, n), lambda i: (jnp.minimum(i, n_tiles - 1), 0))
    prev = lambda n: pl.BlockSpec((tm, n), lambda i: (jnp.maximum(i - 1, 0), 0))
    sds = lambda n, dt: jax.ShapeDtypeStruct((t, n), dt)
    consts = (w_in_bf, drive_matrix(k["ab_re"], k["bb_re"]), drive_matrix(k["ab_im"], k["bb_im"]),
              readout_matrix(k["ac_re"], k["a2c_re"], 1.0), readout_matrix(k["ac_im"], k["a2c_im"], -1.0),
              pair_matrix, a_re, a_im,
              d_skip.reshape(1, width), glu_w.astype(BF16), glu_b.reshape(1, width))
    slab_buf = pltpu.VMEM((n_slab, (tm // 2) * SUBLANES, LANES), F32)
    carry = pltpu.VMEM((n_slab, SUBLANES, LANES), F32)
    return pl.pallas_call(
        functools.partial(_in_proj_s5_kernel, q_scale=LOG2_E * SB_HEAD_DIM ** -0.5,
                          tiles_per_seq=seq // tm),
        grid=(n_tiles + 1,),
        in_specs=[cur(d)] + [_const_spec(c.shape) for c in consts],
        out_specs=[cur(SB_WIDTH), cur(SB_WIDTH), cur(SB_WIDTH), cur(d), cur(d), prev(width)],
        out_shape=[sds(SB_WIDTH, BF16), sds(SB_WIDTH, BF16), sds(SB_WIDTH, BF16),
                   sds(d, BF16), sds(d, BF16), sds(width, BF16)],
        scratch_shapes=[pltpu.VMEM((2, n_ublk, tm, LANES), F32), pltpu.VMEM((n_ublk, tm, LANES), F32),
                        carry, carry, slab_buf, slab_buf],
        compiler_params=pltpu.CompilerParams(dimension_semantics=("arbitrary",),
                                             vmem_limit_bytes=VMEM_LIMIT),
        name="in_proj_s5",
    )(x2, *consts)


def _suffix_sum_matrix():
    j = lax.broadcasted_iota(jnp.int32, (SUFFIX_UNIT, SUFFIX_UNIT), 0)
    s = lax.broadcasted_iota(jnp.int32, (SUFFIX_UNIT, SUFFIX_UNIT), 1)
    return jnp.where(j > s, 1.0, 0.0).astype(BF16)


def _attn_kernel(q_ref, k_ref, v_ref, m_ref, vis_ref, o_ref):
    tq = q_ref.shape[0]
    blk = SUFFIX_UNIT
    n_blk = tq // blk
    i = pl.program_id(2)
    lane = lax.broadcasted_iota(jnp.int32, (1, LANES), 1)
    q2 = q_ref[...]
    zero = jnp.zeros_like(q2)
    heads = (jnp.where(lane < SB_HEAD_DIM, q2, zero), jnp.where(lane >= SB_HEAD_DIM, q2, zero))
    qs = jnp.concatenate([h[b * blk:(b + 1) * blk] for b in range(n_blk) for h in heads], axis=0)
    m = m_ref[...]

    def unit(start, acc, ls, r0, mask=None, gain=None):
        def masked(x):
            if gain is not None:
                x = x * gain
            if mask is None:
                return x
            top = x[:mask.shape[0]] * mask
            return top if mask.shape[0] == x.shape[0] else jnp.concatenate([top, x[mask.shape[0]:]], axis=0)

        ks = k_ref[pl.ds(start, blk), :]
        vs = v_ref[pl.ds(start, blk), :]
        z = lax.dot_general(qs[r0:], ks, (((1,), (1,)), ((), ())), preferred_element_type=F32)
        sp = jnp.maximum(z, jnp.log(1.0 + jnp.exp2(jnp.minimum(z, SP_LINEAR_FROM))) * LOG2_E)
        d = z - sp
        hi = masked(sp).astype(BF16)
        suffix = jnp.dot(hi, m, preferred_element_type=F32)
        ls_in = ls[r0:]
        w = masked(jnp.exp2(d - suffix + jnp.concatenate([ls_in, ls_in], axis=1)))
        acc_out = acc[r0:] + jnp.dot(w.astype(BF16), vs, preferred_element_type=F32)
        ls_out = ls_in - (suffix[:, :1] + hi[:, :1].astype(F32))
        if r0 == 0:
            return acc_out, ls_out
        return (jnp.concatenate([acc[:r0], acc_out], axis=0),
                jnp.concatenate([ls[:r0], ls_out], axis=0))

    def diagonal_units(acc, ls):
        for j in reversed(range(n_blk)):
            acc, ls = unit(pl.multiple_of(i * tq + j * blk, blk), acc, ls, 2 * j * blk,
                           mask=vis_ref[...])
        return acc, ls

    zeros = jnp.zeros((2 * tq, LANES), F32)
    n_older = i * n_blk

    acc, ls = diagonal_units(zeros, zeros)
    prev_start = pl.multiple_of(jnp.maximum(i * tq - blk, 0), blk)
    acc, ls = unit(prev_start, acc, ls, 0, gain=(i > 0).astype(F32))

    def more(c):
        return jnp.logical_and(c[0] < n_older, jnp.max(c[2]) > LS_FLOOR)

    def older_unit(c):
        n, acc, ls = c
        acc, ls = unit(pl.multiple_of((n_older - 1 - n) * blk, blk), acc, ls, 0)
        return n + 1, acc, ls

    _, acc, ls = lax.while_loop(more, older_unit, (jnp.minimum(i, 1), acc, ls))
    per_head = [jnp.concatenate([acc[(2 * b + h) * blk:(2 * b + h + 1) * blk] for b in range(n_blk)],
                                axis=0) for h in range(2)]
    o_ref[...] = jnp.where(lane < SB_HEAD_DIM, per_head[0], per_head[1]).astype(o_ref.dtype)


def _causal_visibility():
    r = lax.broadcasted_iota(jnp.int32, (2 * SUFFIX_UNIT, SUFFIX_UNIT), 0)
    col = lax.broadcasted_iota(jnp.int32, (2 * SUFFIX_UNIT, SUFFIX_UNIT), 1)
    return (col < r % SUFFIX_UNIT).astype(F32)


def _attention(q, k, v, bsz, seq):
    t = q.shape[0]
    tq = ATTN_TQ
    nq = seq // tq
    n_blk = SB_WIDTH // LANES
    qspec = pl.BlockSpec((tq, LANES), lambda b, h, i: (b * nq + i, h))
    kvspec = pl.BlockSpec((seq, LANES), lambda b, h, i: (b, h))
    return pl.pallas_call(
        _attn_kernel,
        grid=(bsz, n_blk, nq),
        in_specs=[qspec, kvspec, kvspec, _const_spec((SUFFIX_UNIT, SUFFIX_UNIT)),
                  _const_spec((2 * SUFFIX_UNIT, SUFFIX_UNIT))],
        out_specs=qspec,
        out_shape=jax.ShapeDtypeStruct((t, SB_WIDTH), BF16),
        compiler_params=pltpu.CompilerParams(
            dimension_semantics=("parallel", "parallel", "arbitrary"),
            vmem_limit_bytes=VMEM_LIMIT),
        name="stickbreak_attention",
    )(q, k, v, _suffix_sum_matrix(), _causal_visibility())


def _merge_ln1_kernel(x_ref, oa_ref, ob_ref, ga_ref, gb_ref, wa_ref, wb_ref, wo_ref,
                      g_ref, b_ref, h_ref, *, alpha):
    sub = x_ref.shape[0] // 2
    for r0 in (0, sub):
        rows = slice(r0, r0 + sub)
        pa = jnp.dot(oa_ref[rows, :], wa_ref[...], preferred_element_type=F32)
        pb = jnp.dot(ob_ref[rows, :], wb_ref[...], preferred_element_type=F32)
        merged = ga_ref[rows, :].astype(F32) * pa + gb_ref[rows, :].astype(F32) * pb
        r = alpha * x_ref[rows, :] + jnp.dot(merged.astype(BF16), wo_ref[...],
                                             preferred_element_type=F32)
        h_ref[rows, :] = _layer_norm(r, g_ref[...], b_ref[...])


def _merge_ln1(x2, o_a, o_b, sg_a, sg_b, w_a, w_b, w_out, ln_g, ln_b, alpha):
    t, d = x2.shape
    tm = ROW_TILE
    row = lambda n: pl.BlockSpec((tm, n), lambda i: (i, 0))
    return pl.pallas_call(
        functools.partial(_merge_ln1_kernel, alpha=alpha),
        grid=(t // tm,),
        in_specs=[row(d), row(o_a.shape[1]), row(o_b.shape[1]), row(d), row(d),
                  _const_spec(w_a.shape), _const_spec(w_b.shape), _const_spec(w_out.shape),
                  _const_spec((1, d)), _const_spec((1, d))],
        out_specs=row(d),
        out_shape=jax.ShapeDtypeStruct((t, d), F32),
        compiler_params=pltpu.CompilerParams(dimension_semantics=("parallel",),
                                             vmem_limit_bytes=VMEM_LIMIT),
        name="merge_ln1",
    )(x2, o_a, o_b, sg_a, sg_b, w_a, w_b, w_out, ln_g.reshape(1, d), ln_b.reshape(1, d))


def _ffn_ple_ln2_kernel(h_ref, p_ref, wg_ref, wu_ref, wd_ref, wpg_ref, wpp_ref,
                        g_ref, b_ref, o_ref, *, alpha, ff_chunk):
    h = h_ref[...]
    hb = h.astype(BF16)
    d_ff = wg_ref.shape[1]
    ffn = jnp.zeros(h.shape, F32)
    for c0 in range(0, d_ff, ff_chunk):
        gate = jnp.dot(hb, wg_ref[:, c0:c0 + ff_chunk], preferred_element_type=F32)
        up = jnp.dot(hb, wu_ref[:, c0:c0 + ff_chunk], preferred_element_type=F32)
        act = (jax.nn.silu(gate) * up).astype(BF16)
        ffn = ffn + jnp.dot(act, wd_ref[c0:c0 + ff_chunk, :], preferred_element_type=F32)
    ple_gate = jax.nn.sigmoid(jnp.dot(hb, wpg_ref[...], preferred_element_type=F32))
    ple = ple_gate * jnp.dot(p_ref[...].astype(BF16), wpp_ref[...], preferred_element_type=F32)
    o_ref[...] = _layer_norm(alpha * h + ffn + ple, g_ref[...], b_ref[...])


def _ffn_ple_ln2(h1, p2, w_gate, w_up, w_down, w_pg, w_pp, ln_g, ln_b, alpha):
    t, d = h1.shape
    tm = ROW_TILE
    d_ff = w_gate.shape[1]
    ff_chunk = 2 * LANES if d_ff % (2 * LANES) == 0 else d_ff
    row = lambda n: pl.BlockSpec((tm, n), lambda i: (i, 0))
    return pl.pallas_call(
        functools.partial(_ffn_ple_ln2_kernel, alpha=alpha, ff_chunk=ff_chunk),
        grid=(t // tm,),
        in_specs=[row(d), row(p2.shape[1]),
                  _const_spec(w_gate.shape), _const_spec(w_up.shape), _const_spec(w_down.shape),
                  _const_spec(w_pg.shape), _const_spec(w_pp.shape),
                  _const_spec((1, d)), _const_spec((1, d))],
        out_specs=row(d),
        out_shape=jax.ShapeDtypeStruct((t, d), F32),
        compiler_params=pltpu.CompilerParams(dimension_semantics=("parallel",),
                                             vmem_limit_bytes=VMEM_LIMIT),
        name="ffn_ple_ln2",
    )(h1, p2, w_gate, w_up, w_down, w_pg, w_pp, ln_g.reshape(1, d), ln_b.reshape(1, d))


def kernel(x, p, w_in, ssm_lambda_re, ssm_lambda_im, ssm_log_dt, ssm_b_re, ssm_b_im, ssm_c_re, ssm_c_im, ssm_d, ssm_glu_w, ssm_glu_b, w_branch_a, w_branch_b, w_out, ln1_g, ln1_b, ffn_w_gate, ffn_w_up, ffn_w_down, ple_w_gate, ple_w_proj, ln2_g, ln2_b):
    bsz, seq, d = x.shape
    depth = w_in.shape[0]
    alpha = (2 * depth) ** 0.25
    t = bsz * seq
    h = x.reshape(t, d)
    for i in range(depth):
        q, k, v, sg_a, sg_b, o_b = _in_proj_s5(
            h, seq, w_in[i].astype(BF16), ssm_lambda_re[i], ssm_lambda_im[i], ssm_log_dt[i],
            ssm_b_re[i], ssm_b_im[i], ssm_c_re[i], ssm_c_im[i], ssm_d[i], ssm_glu_w[i], ssm_glu_b[i])
        o_a = _attention(q, k, v, bsz, seq)
        h = _merge_ln1(h, o_a, o_b, sg_a, sg_b, w_branch_a[i].astype(BF16), w_branch_b[i].astype(BF16),
                       w_out[i].astype(BF16), ln1_g[i], ln1_b[i], alpha)
        h = _ffn_ple_ln2(h, p[i].reshape(t, -1), ffn_w_gate[i].astype(BF16), ffn_w_up[i].astype(BF16),
                         ffn_w_down[i].astype(BF16), ple_w_gate[i].astype(BF16),
                         ple_w_proj[i].astype(BF16), ln2_g[i], ln2_b[i], alpha)
    return h.reshape(bsz, seq, d)
```

```python
import functools

import jax
import jax.numpy as jnp
from jax import lax
from jax.experimental import pallas as pl
from jax.experimental.pallas import tpu as pltpu

F32 = jnp.float32
BF16 = jnp.bfloat16

LN_EPS = 1e-5
SB_HEADS = 8
SB_HEAD_DIM = 64
SB_WIDTH = SB_HEADS * SB_HEAD_DIM
SSM_GROUP = 16
SSM_STATE = 64

LANES = 128
SUBLANES = 8
SUFFIX_UNIT = 2 * LANES
LOG2_E = 1.4426950408889634
SP_LINEAR_FROM = 126.0
LS_FLOOR = -200.0
VMEM_LIMIT = 56 * 1024 * 1024

ROW_TILE = 512
ATTN_TQ = 512


def _const_spec(shape):
    nd = len(shape)
    return pl.BlockSpec(shape, lambda *_: (0,) * nd, pipeline_mode=pl.Buffered(1))


def _layer_norm(r, g, b):
    mu = jnp.mean(r, axis=-1, keepdims=True)
    d = r - mu
    var = jnp.mean(d * d, axis=-1, keepdims=True)
    return d * lax.rsqrt(var + LN_EPS) * g + b


def _s5_discretize_kernel(lre_ref, lim_ref, ldt_ref, bre_ref, bim_ref, cre_ref, cim_ref,
                          a2re_ref, a2im_ref, bbre_ref, bbim_ref, abre_ref, abim_ref,
                          acre_ref, acim_ref, a2cre_ref, a2cim_ref):
    lam_re = lre_ref[...]
    lam_im = lim_ref[...]
    dt = jnp.exp(ldt_ref[...])
    mag = jnp.exp(lam_re * dt)
    a_re = mag * jnp.cos(lam_im * dt)
    a_im = mag * jnp.sin(lam_im * dt)
    den = lam_re * lam_re + lam_im * lam_im
    n_re = a_re - 1.0
    coef_re = (n_re * lam_re + a_im * lam_im) / den
    coef_im = (a_im * lam_re - n_re * lam_im) / den
    b_re = bre_ref[...]
    b_im = bim_ref[...]
    bb_re = coef_re * b_re - coef_im * b_im
    bb_im = coef_re * b_im + coef_im * b_re
    a2_re = a_re * a_re - a_im * a_im
    a2_im = 2.0 * a_re * a_im
    c_re = cre_ref[...]
    c_im = cim_ref[...]
    a2re_ref[...] = a2_re
    a2im_ref[...] = a2_im
    bbre_ref[...] = bb_re
    bbim_ref[...] = bb_im
    abre_ref[...] = a_re * bb_re - a_im * bb_im
    abim_ref[...] = a_re * bb_im + a_im * bb_re
    acre_ref[...] = a_re * c_re - a_im * c_im
    acim_ref[...] = a_re * c_im + a_im * c_re
    a2cre_ref[...] = a2_re * c_re - a2_im * c_im
    a2cim_ref[...] = a2_re * c_im + a2_im * c_re


def _s5_pair_terms_kernel(ccre_ref, ccim_ref, bbre_ref, bbim_ref, abre_ref, abim_ref, sel_ref,
                          cb_ref, cab_ref):
    sel = sel_ref[...]
    for c_out in range(ccre_ref.shape[0]):
        c_re = ccre_ref[c_out]
        c_im = ccim_ref[c_out]
        cb_ref[c_out] = jnp.dot(c_re * bbre_ref[...] - c_im * bbim_ref[...], sel,
                                preferred_element_type=F32)
        cab_ref[c_out] = jnp.dot(c_re * abre_ref[...] - c_im * abim_ref[...], sel,
                                 preferred_element_type=F32)


def _s5_constants(lam_re, lam_im, log_dt, b_re, b_im, c_re, c_im):
    g, p = lam_re.shape
    c = b_re.shape[-1]
    rep = lambda a: jnp.repeat(a, c, axis=1)
    flat = lambda a: a.reshape(g, p * c)
    ldt = jnp.broadcast_to(log_dt[:, None], (g, p * c))
    ct_re, ct_im = c_re.transpose(0, 2, 1), c_im.transpose(0, 2, 1)
    shp = jax.ShapeDtypeStruct((g, p * c), F32)
    a2_re, a2_im, bb_re, bb_im, ab_re, ab_im, ac_re, ac_im, a2c_re, a2c_im = pl.pallas_call(
        _s5_discretize_kernel, out_shape=(shp,) * 10, name="s5_discretize",
    )(rep(lam_re), rep(lam_im), ldt, flat(b_re), flat(b_im), flat(ct_re), flat(ct_im))
    cc = lambda m: jnp.repeat(m.transpose(1, 0, 2), c, axis=2)
    sel = (lax.broadcasted_iota(jnp.int32, (p * c, c), 0) % c
           == lax.broadcasted_iota(jnp.int32, (p * c, c), 1)).astype(F32)
    pair = jax.ShapeDtypeStruct((c, g, c), F32)
    cb, cab = pl.pallas_call(
        _s5_pair_terms_kernel, out_shape=(pair, pair), name="s5_pair_terms",
    )(cc(c_re), cc(c_im), bb_re, bb_im, ab_re, ab_im, sel)
    unflat = lambda a: a.reshape(g, p, c)
    return dict(a2_re=a2_re[:, ::c], a2_im=a2_im[:, ::c],
                bb_re=unflat(bb_re), bb_im=unflat(bb_im), ab_re=unflat(ab_re), ab_im=unflat(ab_im),
                ac_re=unflat(ac_re), ac_im=unflat(ac_im), a2c_re=unflat(a2c_re), a2c_im=unflat(a2c_im),
                cb=cb.transpose(1, 0, 2), cab=cab.transpose(1, 0, 2))


def _in_proj_s5_kernel(x_ref, w_ref, abre_ref, abim_ref, bbre_ref, bbim_ref,
                       acre_ref, acim_ref, a2cre_ref, a2cim_ref, cb_ref, cab_ref, are_ref, aim_ref,
                       d_ref, gw_ref, gbias_ref,
                       q_ref, k_ref, v_ref, ga_ref, gb_ref, ob_ref,
                       u_buf, y_buf, cre_st, cim_st, sre_buf, sim_buf, *, q_scale, tiles_per_seq):
    i = pl.program_id(0)
    tm, d_model = x_ref.shape
    n_ublk = u_buf.shape[1]
    width = n_ublk * LANES
    n_pair = tm // 2
    n_chunk = bbre_ref.shape[1] // LANES
    n_slab = n_chunk // SUBLANES
    ch_per_ublk = n_chunk // n_ublk

    @pl.when(i == 0)
    def _():
        u_buf[...] = jnp.zeros_like(u_buf)
        cre_st[...] = jnp.zeros_like(cre_st)
        cim_st[...] = jnp.zeros_like(cim_st)

    slot = i % 2

    def u_pairs(blk):
        even = u_buf[1 - slot, blk, pl.ds(0, n_pair, stride=2), :]
        odd = u_buf[1 - slot, blk, pl.ds(1, n_pair, stride=2), :]
        return jnp.concatenate([even, odd], axis=1).astype(BF16)

    def slab_rows(c):
        return pl.ds(c % SUBLANES, n_pair, stride=SUBLANES)

    def drive():
        for blk in range(n_ublk):
            up = u_pairs(blk)
            rows = slice(blk * LANES, (blk + 1) * LANES)
            cols = slice(blk * ch_per_ublk * LANES, (blk + 1) * ch_per_ublk * LANES)
            for ab_ref, bb_ref, buf in ((abre_ref, bbre_ref, sre_buf), (abim_ref, bbim_ref, sim_buf)):
                w2 = jnp.concatenate([ab_ref[rows, cols], bb_ref[rows, cols]], axis=0)
                val = jnp.dot(up, w2, preferred_element_type=F32)
                for e in range(ch_per_ublk):
                    c = blk * ch_per_ublk + e
                    buf[c // SUBLANES, slab_rows(c), :] = val[:, e * LANES:(e + 1) * LANES]

    a_re = [are_ref[k] for k in range(n_slab)]
    a_im = [aim_ref[k] for k in range(n_slab)]

    def scan(carry):
        for t in range(n_pair):
            tok = pl.ds(t * SUBLANES, SUBLANES)
            new = []
            for idx in range(n_slab):
                s_re, s_im = carry[idx]
                n_re = a_re[idx] * s_re - a_im[idx] * s_im + sre_buf[idx, tok, :]
                n_im = a_re[idx] * s_im + a_im[idx] * s_re + sim_buf[idx, tok, :]
                sre_buf[idx, tok, :] = s_re
                sim_buf[idx, tok, :] = s_im
                new.append((n_re, n_im))
            carry = new
        return carry

    def readout():
        for blk in range(n_ublk):
            cols = slice(blk * LANES, (blk + 1) * LANES)
            cb = cb_ref[cols, cols]
            intra = jnp.concatenate([jnp.concatenate([cb, cab_ref[cols, cols]], axis=1),
                                     jnp.concatenate([jnp.zeros_like(cb), cb], axis=1)], axis=0)
            acc = jnp.dot(u_pairs(blk), intra, preferred_element_type=F32)
            for half in range(ch_per_ublk // 2):
                c0 = blk * ch_per_ublk + 2 * half
                rows = slice(c0 * LANES, (c0 + 2) * LANES)

                def chunk_pair(buf):
                    parts = [buf[(c0 + e) // SUBLANES, slab_rows(c0 + e), :] for e in range(2)]
                    return jnp.concatenate(parts, axis=1).astype(BF16)

                r_re = jnp.concatenate([acre_ref[rows, cols], a2cre_ref[rows, cols]], axis=1)
                r_im = jnp.concatenate([acim_ref[rows, cols], a2cim_ref[rows, cols]], axis=1)
                acc = acc + jnp.dot(chunk_pair(sre_buf), r_re, preferred_element_type=F32)
                acc = acc - jnp.dot(chunk_pair(sim_buf), r_im, preferred_element_type=F32)
            y_buf[blk, pl.ds(0, n_pair, stride=2), :] = acc[:, :LANES]
            y_buf[blk, pl.ds(1, n_pair, stride=2), :] = acc[:, LANES:]
        y = jnp.concatenate([y_buf[blk] + d_ref[:, blk * LANES:(blk + 1) * LANES] * u_buf[1 - slot, blk]
                             for blk in range(n_ublk)], axis=1)
        y = jax.nn.gelu(y)
        gate = jnp.dot(y.astype(BF16), gw_ref[...], preferred_element_type=F32) + gbias_ref[...]
        ob_ref[...] = (y * jax.nn.sigmoid(gate)).astype(ob_ref.dtype)

    opens_sequence = (i + tiles_per_seq - 1) % tiles_per_seq == 0
    carry = [(jnp.where(opens_sequence, 0.0, cre_st[idx]), jnp.where(opens_sequence, 0.0, cim_st[idx]))
             for idx in range(n_slab)]

    xb = x_ref[...].astype(BF16)
    w = SB_WIDTH
    g0 = 3 * w + width
    half = d_model // 2

    def project(c0, n):
        return jnp.dot(xb, w_ref[:, c0:c0 + n], preferred_element_type=F32)

    def gate(out_ref, c0, lo):
        out_ref[:, lo:lo + half] = jax.nn.sigmoid(project(c0 + lo, half)).astype(BF16)

    q_ref[...] = (project(0, w) * q_scale).astype(BF16)
    drive()
    carry = scan(carry)
    k_ref[...] = project(w, w).astype(BF16)
    v_ref[...] = project(2 * w, w).astype(BF16)
    gate(ga_ref, g0, 0)
    readout()
    gate(ga_ref, g0, half)
    gate(gb_ref, g0 + d_model, 0)
    gate(gb_ref, g0 + d_model, half)
    for idx in range(n_slab):
        cre_st[idx] = carry[idx][0]
        cim_st[idx] = carry[idx][1]
    u_new = project(3 * w, width)
    for blk in range(n_ublk):
        u_buf[slot, blk] = u_new[:, blk * LANES:(blk + 1) * LANES]


def _block_diag(blocks, dtype):
    g, r, c = blocks.shape
    wide = blocks.transpose(1, 0, 2).reshape(r, g * c)
    row_g = lax.broadcasted_iota(jnp.int32, (g * r, g * c), 0) // r
    col_g = lax.broadcasted_iota(jnp.int32, (g * r, g * c), 1) // c
    return jnp.where(row_g == col_g, jnp.tile(wide, (g, 1)), 0.0).astype(dtype)


def _in_proj_s5(x2, seq, w_in_bf, lam_re, lam_im, log_dt, b_re, b_im, c_re, c_im, d_skip, glu_w, glu_b):
    t, d = x2.shape
    width = d_skip.shape[0]
    n_ublk = width // LANES
    k = _s5_constants(lam_re, lam_im, log_dt, b_re, b_im, c_re, c_im)
    n_state = k["a2_re"].size
    n_slab = n_state // (SUBLANES * LANES)
    to_states = lambda m: _block_diag(m.transpose(0, 2, 1), BF16)
    to_y = lambda m: _block_diag(m, BF16)
    a_re = k["a2_re"].reshape(n_slab, SUBLANES, LANES)
    a_im = k["a2_im"].reshape(n_slab, SUBLANES, LANES)
    tm = ROW_TILE
    n_tiles = t // tm
    cur = lambda n: pl.BlockSpec((tm, n), lambda i: (jnp.minimum(i, n_tiles - 1), 0))
    prev = lambda n: pl.BlockSpec((tm, n), lambda i: (jnp.maximum(i - 1, 0), 0))
    sds = lambda n, dt: jax.ShapeDtypeStruct((t, n), dt)
    consts = (w_in_bf, to_states(k["ab_re"]), to_states(k["ab_im"]), to_states(k["bb_re"]), to_states(k["bb_im"]),
              to_y(k["ac_re"]), to_y(k["ac_im"]), to_y(k["a2c_re"]), to_y(k["a2c_im"]),
              to_states(k["cb"]), to_states(k["cab"]), a_re, a_im,
              d_skip.reshape(1, width), glu_w.astype(BF16), glu_b.reshape(1, width))
    slab_buf = pltpu.VMEM((n_slab, (tm // 2) * SUBLANES, LANES), F32)
    carry = pltpu.VMEM((n_slab, SUBLANES, LANES), F32)
    return pl.pallas_call(
        functools.partial(_in_proj_s5_kernel, q_scale=LOG2_E * SB_HEAD_DIM ** -0.5,
                          tiles_per_seq=seq // tm),
        grid=(n_tiles + 1,),
        in_specs=[cur(d)] + [_const_spec(c.shape) for c in consts],
        out_specs=[cur(SB_WIDTH), cur(SB_WIDTH), cur(SB_WIDTH), cur(d), cur(d), prev(width)],
        out_shape=[sds(SB_WIDTH, BF16), sds(SB_WIDTH, BF16), sds(SB_WIDTH, BF16),
                   sds(d, BF16), sds(d, BF16), sds(width, BF16)],
        scratch_shapes=[pltpu.VMEM((2, n_ublk, tm, LANES), F32), pltpu.VMEM((n_ublk, tm, LANES), F32),
                        carry, carry, slab_buf, slab_buf],
        compiler_params=pltpu.CompilerParams(dimension_semantics=("arbitrary",),
                                             vmem_limit_bytes=VMEM_LIMIT),
        name="in_proj_s5",
    )(x2, *consts)


def _suffix_sum_matrix():
    j = lax.broadcasted_iota(jnp.int32, (SUFFIX_UNIT, SUFFIX_UNIT), 0)
    s = lax.broadcasted_iota(jnp.int32, (SUFFIX_UNIT, SUFFIX_UNIT), 1)
    return jnp.where(j > s, 1.0, 0.0).astype(BF16)


def _attn_kernel(q_ref, k_ref, v_ref, m_ref, vis_ref, o_ref):
    tq = q_ref.shape[0]
    blk = SUFFIX_UNIT
    n_blk = tq // blk
    i = pl.program_id(2)
    lane = lax.broadcasted_iota(jnp.int32, (1, LANES), 1)
    q2 = q_ref[...]
    zero = jnp.zeros_like(q2)
    heads = (jnp.where(lane < SB_HEAD_DIM, q2, zero), jnp.where(lane >= SB_HEAD_DIM, q2, zero))
    qs = jnp.concatenate([h[b * blk:(b + 1) * blk] for b in range(n_blk) for h in heads], axis=0)
    m = m_ref[...]

    def unit(start, acc, ls, r0, mask=None, gain=None):
        def masked(x):
            if gain is not None:
                x = x * gain
            if mask is None:
                return x
            top = x[:mask.shape[0]] * mask
            return top if mask.shape[0] == x.shape[0] else jnp.concatenate([top, x[mask.shape[0]:]], axis=0)

        ks = k_ref[pl.ds(start, blk), :]
        vs = v_ref[pl.ds(start, blk), :]
        z = lax.dot_general(qs[r0:], ks, (((1,), (1,)), ((), ())), preferred_element_type=F32)
        sp = jnp.maximum(z, jnp.log(1.0 + jnp.exp2(jnp.minimum(z, SP_LINEAR_FROM))) * LOG2_E)
        d = z - sp
        hi = masked(sp).astype(BF16)
        suffix = jnp.dot(hi, m, preferred_element_type=F32)
        ls_in = ls[r0:]
        w = masked(jnp.exp2(d - suffix + jnp.concatenate([ls_in, ls_in], axis=1)))
        acc_out = acc[r0:] + jnp.dot(w.astype(BF16), vs, preferred_element_type=F32)
        ls_out = ls_in - (suffix[:, :1] + hi[:, :1].astype(F32))
        if r0 == 0:
            return acc_out, ls_out
        return (jnp.concatenate([acc[:r0], acc_out], axis=0),
                jnp.concatenate([ls[:r0], ls_out], axis=0))

    def diagonal_units(acc, ls):
        for j in reversed(range(n_blk)):
            acc, ls = unit(pl.multiple_of(i * tq + j * blk, blk), acc, ls, 2 * j * blk,
                           mask=vis_ref[...])
        return acc, ls

    zeros = jnp.zeros((2 * tq, LANES), F32)
    n_older = i * n_blk

    acc, ls = diagonal_units(zeros, zeros)
    prev_start = pl.multiple_of(jnp.maximum(i * tq - blk, 0), blk)
    acc, ls = unit(prev_start, acc, ls, 0, gain=(i > 0).astype(F32))

    def more(c):
        return jnp.logical_and(c[0] < n_older, jnp.max(c[2]) > LS_FLOOR)

    def older_unit(c):
        n, acc, ls = c
        acc, ls = unit(pl.multiple_of((n_older - 1 - n) * blk, blk), acc, ls, 0)
        return n + 1, acc, ls

    _, acc, ls = lax.while_loop(more, older_unit, (jnp.minimum(i, 1), acc, ls))
    per_head = [jnp.concatenate([acc[(2 * b + h) * blk:(2 * b + h + 1) * blk] for b in range(n_blk)],
                                axis=0) for h in range(2)]
    o_ref[...] = jnp.where(lane < SB_HEAD_DIM, per_head[0], per_head[1]).astype(o_ref.dtype)


def _causal_visibility():
    r = lax.broadcasted_iota(jnp.int32, (2 * SUFFIX_UNIT, SUFFIX_UNIT), 0)
    col = lax.broadcasted_iota(jnp.int32, (2 * SUFFIX_UNIT, SUFFIX_UNIT), 1)
    return (col < r % SUFFIX_UNIT).astype(F32)


def _attention(q, k, v, bsz, seq):
    t = q.shape[0]
    tq = ATTN_TQ
    nq = seq // tq
    n_blk = SB_WIDTH // LANES
    qspec = pl.BlockSpec((tq, LANES), lambda b, h, i: (b * nq + i, h))
    kvspec = pl.BlockSpec((seq, LANES), lambda b, h, i: (b, h))
    return pl.pallas_call(
        _attn_kernel,
        grid=(bsz, n_blk, nq),
        in_specs=[qspec, kvspec, kvspec, _const_spec((SUFFIX_UNIT, SUFFIX_UNIT)),
                  _const_spec((2 * SUFFIX_UNIT, SUFFIX_UNIT))],
        out_specs=qspec,
        out_shape=jax.ShapeDtypeStruct((t, SB_WIDTH), BF16),
        compiler_params=pltpu.CompilerParams(
            dimension_semantics=("parallel", "parallel", "arbitrary"),
            vmem_limit_bytes=VMEM_LIMIT),
        name="stickbreak_attention",
    )(q, k, v, _suffix_sum_matrix(), _causal_visibility())


def _merge_ln1_kernel(x_ref, oa_ref, ob_ref, ga_ref, gb_ref, wa_ref, wb_ref, wo_ref,
                      g_ref, b_ref, h_ref, *, alpha):
    sub = x_ref.shape[0] // 2
    for r0 in (0, sub):
        rows = slice(r0, r0 + sub)
        pa = jnp.dot(oa_ref[rows, :], wa_ref[...], preferred_element_type=F32)
        pb = jnp.dot(ob_ref[rows, :], wb_ref[...], preferred_element_type=F32)
        merged = ga_ref[rows, :].astype(F32) * pa + gb_ref[rows, :].astype(F32) * pb
        r = alpha * x_ref[rows, :] + jnp.dot(merged.astype(BF16), wo_ref[...],
                                             preferred_element_type=F32)
        h_ref[rows, :] = _layer_norm(r, g_ref[...], b_ref[...])


def _merge_ln1(x2, o_a, o_b, sg_a, sg_b, w_a, w_b, w_out, ln_g, ln_b, alpha):
    t, d = x2.shape
    tm = ROW_TILE
    row = lambda n: pl.BlockSpec((tm, n), lambda i: (i, 0))
    return pl.pallas_call(
        functools.partial(_merge_ln1_kernel, alpha=alpha),
        grid=(t // tm,),
        in_specs=[row(d), row(o_a.shape[1]), row(o_b.shape[1]), row(d), row(d),
                  _const_spec(w_a.shape), _const_spec(w_b.shape), _const_spec(w_out.shape),
                  _const_spec((1, d)), _const_spec((1, d))],
        out_specs=row(d),
        out_shape=jax.ShapeDtypeStruct((t, d), F32),
        compiler_params=pltpu.CompilerParams(dimension_semantics=("parallel",),
                                             vmem_limit_bytes=VMEM_LIMIT),
        name="merge_ln1",
    )(x2, o_a, o_b, sg_a, sg_b, w_a, w_b, w_out, ln_g.reshape(1, d), ln_b.reshape(1, d))


def _ffn_ple_ln2_kernel(h_ref, p_ref, wg_ref, wu_ref, wd_ref, wpg_ref, wpp_ref,
                        g_ref, b_ref, o_ref, *, alpha, ff_chunk):
    h = h_ref[...]
    hb = h.astype(BF16)
    d_ff = wg_ref.shape[1]
    ffn = jnp.zeros(h.shape, F32)
    for c0 in range(0, d_ff, ff_chunk):
        gate = jnp.dot(hb, wg_ref[:, c0:c0 + ff_chunk], preferred_element_type=F32)
        up = jnp.dot(hb, wu_ref[:, c0:c0 + ff_chunk], preferred_element_type=F32)
        act = (jax.nn.silu(gate) * up).astype(BF16)
        ffn = ffn + jnp.dot(act, wd_ref[c0:c0 + ff_chunk, :], preferred_element_type=F32)
    ple_gate = jax.nn.sigmoid(jnp.dot(hb, wpg_ref[...], preferred_element_type=F32))
    ple = ple_gate * jnp.dot(p_ref[...].astype(BF16), wpp_ref[...], preferred_element_type=F32)
    o_ref[...] = _layer_norm(alpha * h + ffn + ple, g_ref[...], b_ref[...])


def _ffn_ple_ln2(h1, p2, w_gate, w_up, w_down, w_pg, w_pp, ln_g, ln_b, alpha):
    t, d = h1.shape
    tm = ROW_TILE
    d_ff = w_gate.shape[1]
    ff_chunk = 2 * LANES if d_ff % (2 * LANES) == 0 else d_ff
    row = lambda n: pl.BlockSpec((tm, n), lambda i: (i, 0))
    return pl.pallas_call(
        functools.partial(_ffn_ple_ln2_kernel, alpha=alpha, ff_chunk=ff_chunk),
        grid=(t // tm,),
        in_specs=[row(d), row(p2.shape[1]),
                  _const_spec(w_gate.shape), _const_spec(w_up.shape), _const_spec(w_down.shape),
                  _const_spec(w_pg.shape), _const_spec(w_pp.shape),
                  _const_spec((1, d)), _const_spec((1, d))],
        out_specs=row(d),
        out_shape=jax.ShapeDtypeStruct((t, d), F32),
        compiler_params=pltpu.CompilerParams(dimension_semantics=("parallel",),
                                             vmem_limit_bytes=VMEM_LIMIT),
        name="ffn_ple_ln2",
    )(h1, p2, w_gate, w_up, w_down, w_pg, w_pp, ln_g.reshape(1, d), ln_b.reshape(1, d))


def kernel(x, p, w_in, ssm_lambda_re, ssm_lambda_im, ssm_log_dt, ssm_b_re, ssm_b_im, ssm_c_re, ssm_c_im, ssm_d, ssm_glu_w, ssm_glu_b, w_branch_a, w_branch_b, w_out, ln1_g, ln1_b, ffn_w_gate, ffn_w_up, ffn_w_down, ple_w_gate, ple_w_proj, ln2_g, ln2_b):
    bsz, seq, d = x.shape
    depth = w_in.shape[0]
    alpha = (2 * depth) ** 0.25
    t = bsz * seq
    h = x.reshape(t, d)
    for i in range(depth):
        q, k, v, sg_a, sg_b, o_b = _in_proj_s5(
            h, seq, w_in[i].astype(BF16), ssm_lambda_re[i], ssm_lambda_im[i], ssm_log_dt[i],
            ssm_b_re[i], ssm_b_im[i], ssm_c_re[i], ssm_c_im[i], ssm_d[i], ssm_glu_w[i], ssm_glu_b[i])
        o_a = _attention(q, k, v, bsz, seq)
        h = _merge_ln1(h, o_a, o_b, sg_a, sg_b, w_branch_a[i].astype(BF16), w_branch_b[i].astype(BF16),
                       w_out[i].astype(BF16), ln1_g[i], ln1_b[i], alpha)
        h = _ffn_ple_ln2(h, p[i].reshape(t, -1), ffn_w_gate[i].astype(BF16), ffn_w_up[i].astype(BF16),
                         ffn_w_down[i].astype(BF16), ple_w_gate[i].astype(BF16),
                         ple_w_proj[i].astype(BF16), ln2_g[i], ln2_b[i], alpha)
    return h.reshape(bsz, seq, d)
```

```python
import functools

import jax
import jax.numpy as jnp
from jax import lax
from jax.experimental import pallas as pl
from jax.experimental.pallas import tpu as pltpu

F32 = jnp.float32
BF16 = jnp.bfloat16

LN_EPS = 1e-5
SB_HEADS = 8
SB_HEAD_DIM = 64
SB_WIDTH = SB_HEADS * SB_HEAD_DIM
SSM_GROUP = 16
SSM_STATE = 64

LANES = 128
SUBLANES = 8
SUFFIX_UNIT = 2 * LANES
LOG2_E = 1.4426950408889634
SP_LINEAR_FROM = 126.0
LS_FLOOR = -200.0
VMEM_LIMIT = 56 * 1024 * 1024

ROW_TILE = 512
ATTN_TQ = 512
ATTN_PAIRS = 2


def _const_spec(shape):
    nd = len(shape)
    return pl.BlockSpec(shape, lambda *_: (0,) * nd, pipeline_mode=pl.Buffered(1))


def _layer_norm(r, g, b):
    mu = jnp.mean(r, axis=-1, keepdims=True)
    d = r - mu
    var = jnp.mean(d * d, axis=-1, keepdims=True)
    return d * lax.rsqrt(var + LN_EPS) * g + b


def _s5_discretize_kernel(lre_ref, lim_ref, ldt_ref, bre_ref, bim_ref, cre_ref, cim_ref,
                          a2re_ref, a2im_ref, bbre_ref, bbim_ref, abre_ref, abim_ref,
                          acre_ref, acim_ref, a2cre_ref, a2cim_ref):
    lam_re = lre_ref[...]
    lam_im = lim_ref[...]
    dt = jnp.exp(ldt_ref[...])
    mag = jnp.exp(lam_re * dt)
    a_re = mag * jnp.cos(lam_im * dt)
    a_im = mag * jnp.sin(lam_im * dt)
    den = lam_re * lam_re + lam_im * lam_im
    n_re = a_re - 1.0
    coef_re = (n_re * lam_re + a_im * lam_im) / den
    coef_im = (a_im * lam_re - n_re * lam_im) / den
    b_re = bre_ref[...]
    b_im = bim_ref[...]
    bb_re = coef_re * b_re - coef_im * b_im
    bb_im = coef_re * b_im + coef_im * b_re
    a2_re = a_re * a_re - a_im * a_im
    a2_im = 2.0 * a_re * a_im
    c_re = cre_ref[...]
    c_im = cim_ref[...]
    a2re_ref[...] = a2_re
    a2im_ref[...] = a2_im
    bbre_ref[...] = bb_re
    bbim_ref[...] = bb_im
    abre_ref[...] = a_re * bb_re - a_im * bb_im
    abim_ref[...] = a_re * bb_im + a_im * bb_re
    acre_ref[...] = a_re * c_re - a_im * c_im
    acim_ref[...] = a_re * c_im + a_im * c_re
    a2cre_ref[...] = a2_re * c_re - a2_im * c_im
    a2cim_ref[...] = a2_re * c_im + a2_im * c_re


def _s5_pair_terms_kernel(ccre_ref, ccim_ref, bbre_ref, bbim_ref, abre_ref, abim_ref, sel_ref,
                          cb_ref, cab_ref):
    sel = sel_ref[...]
    for c_out in range(ccre_ref.shape[0]):
        c_re = ccre_ref[c_out]
        c_im = ccim_ref[c_out]
        cb_ref[c_out] = jnp.dot(c_re * bbre_ref[...] - c_im * bbim_ref[...], sel,
                                preferred_element_type=F32)
        cab_ref[c_out] = jnp.dot(c_re * abre_ref[...] - c_im * abim_ref[...], sel,
                                 preferred_element_type=F32)


def _s5_constants(lam_re, lam_im, log_dt, b_re, b_im, c_re, c_im):
    g, p = lam_re.shape
    c = b_re.shape[-1]
    rep = lambda a: jnp.repeat(a, c, axis=1)
    flat = lambda a: a.reshape(g, p * c)
    ldt = jnp.broadcast_to(log_dt[:, None], (g, p * c))
    ct_re, ct_im = c_re.transpose(0, 2, 1), c_im.transpose(0, 2, 1)
    shp = jax.ShapeDtypeStruct((g, p * c), F32)
    a2_re, a2_im, bb_re, bb_im, ab_re, ab_im, ac_re, ac_im, a2c_re, a2c_im = pl.pallas_call(
        _s5_discretize_kernel, out_shape=(shp,) * 10, name="s5_discretize",
    )(rep(lam_re), rep(lam_im), ldt, flat(b_re), flat(b_im), flat(ct_re), flat(ct_im))
    cc = lambda m: jnp.repeat(m.transpose(1, 0, 2), c, axis=2)
    sel = (lax.broadcasted_iota(jnp.int32, (p * c, c), 0) % c
           == lax.broadcasted_iota(jnp.int32, (p * c, c), 1)).astype(F32)
    pair = jax.ShapeDtypeStruct((c, g, c), F32)
    cb, cab = pl.pallas_call(
        _s5_pair_terms_kernel, out_shape=(pair, pair), name="s5_pair_terms",
    )(cc(c_re), cc(c_im), bb_re, bb_im, ab_re, ab_im, sel)
    unflat = lambda a: a.reshape(g, p, c)
    return dict(a2_re=a2_re[:, ::c], a2_im=a2_im[:, ::c],
                bb_re=unflat(bb_re), bb_im=unflat(bb_im), ab_re=unflat(ab_re), ab_im=unflat(ab_im),
                ac_re=unflat(ac_re), ac_im=unflat(ac_im), a2c_re=unflat(a2c_re), a2c_im=unflat(a2c_im),
                cb=cb.transpose(1, 0, 2), cab=cab.transpose(1, 0, 2))


def _in_proj_s5_kernel(x_ref, w_ref, abre_ref, abim_ref, bbre_ref, bbim_ref,
                       acre_ref, acim_ref, a2cre_ref, a2cim_ref, cb_ref, cab_ref, are_ref, aim_ref,
                       d_ref, gw_ref, gbias_ref,
                       q_ref, k_ref, v_ref, ga_ref, gb_ref, ob_ref,
                       u_buf, y_buf, cre_st, cim_st, sre_buf, sim_buf, *, q_scale, tiles_per_seq):
    i = pl.program_id(0)
    tm, d_model = x_ref.shape
    n_ublk = u_buf.shape[1]
    width = n_ublk * LANES
    n_pair = tm // 2
    n_chunk = bbre_ref.shape[1] // LANES
    n_slab = n_chunk // SUBLANES
    ch_per_ublk = n_chunk // n_ublk

    @pl.when(i == 0)
    def _():
        u_buf[...] = jnp.zeros_like(u_buf)
        cre_st[...] = jnp.zeros_like(cre_st)
        cim_st[...] = jnp.zeros_like(cim_st)

    slot = i % 2

    def u_pairs(blk):
        even = u_buf[1 - slot, blk, pl.ds(0, n_pair, stride=2), :]
        odd = u_buf[1 - slot, blk, pl.ds(1, n_pair, stride=2), :]
        return jnp.concatenate([even, odd], axis=1).astype(BF16)

    def slab_rows(c):
        return pl.ds(c % SUBLANES, n_pair, stride=SUBLANES)

    def drive():
        for blk in range(n_ublk):
            up = u_pairs(blk)
            rows = slice(blk * LANES, (blk + 1) * LANES)
            cols = slice(blk * ch_per_ublk * LANES, (blk + 1) * ch_per_ublk * LANES)
            for ab_ref, bb_ref, buf in ((abre_ref, bbre_ref, sre_buf), (abim_ref, bbim_ref, sim_buf)):
                w2 = jnp.concatenate([ab_ref[rows, cols], bb_ref[rows, cols]], axis=0)
                val = jnp.dot(up, w2, preferred_element_type=F32)
                for e in range(ch_per_ublk):
                    c = blk * ch_per_ublk + e
                    buf[c // SUBLANES, slab_rows(c), :] = val[:, e * LANES:(e + 1) * LANES]

    a_re = [are_ref[k] for k in range(n_slab)]
    a_im = [aim_ref[k] for k in range(n_slab)]

    def scan(carry):
        for t in range(n_pair):
            tok = pl.ds(t * SUBLANES, SUBLANES)
            new = []
            for idx in range(n_slab):
                s_re, s_im = carry[idx]
                n_re = a_re[idx] * s_re - a_im[idx] * s_im + sre_buf[idx, tok, :]
                n_im = a_re[idx] * s_im + a_im[idx] * s_re + sim_buf[idx, tok, :]
                sre_buf[idx, tok, :] = s_re
                sim_buf[idx, tok, :] = s_im
                new.append((n_re, n_im))
            carry = new
        return carry

    def readout():
        for blk in range(n_ublk):
            cols = slice(blk * LANES, (blk + 1) * LANES)
            cb = cb_ref[cols, cols]
            intra = jnp.concatenate([jnp.concatenate([cb, cab_ref[cols, cols]], axis=1),
                                     jnp.concatenate([jnp.zeros_like(cb), cb], axis=1)], axis=0)
            acc = jnp.dot(u_pairs(blk), intra, preferred_element_type=F32)
            for half in range(ch_per_ublk // 2):
                c0 = blk * ch_per_ublk + 2 * half
                rows = slice(c0 * LANES, (c0 + 2) * LANES)

                def chunk_pair(buf):
                    parts = [buf[(c0 + e) // SUBLANES, slab_rows(c0 + e), :] for e in range(2)]
                    return jnp.concatenate(parts, axis=1).astype(BF16)

                r_re = jnp.concatenate([acre_ref[rows, cols], a2cre_ref[rows, cols]], axis=1)
                r_im = jnp.concatenate([acim_ref[rows, cols], a2cim_ref[rows, cols]], axis=1)
                acc = acc + jnp.dot(chunk_pair(sre_buf), r_re, preferred_element_type=F32)
                acc = acc - jnp.dot(chunk_pair(sim_buf), r_im, preferred_element_type=F32)
            y_buf[blk, pl.ds(0, n_pair, stride=2), :] = acc[:, :LANES]
            y_buf[blk, pl.ds(1, n_pair, stride=2), :] = acc[:, LANES:]
        y = jnp.concatenate([y_buf[blk] + d_ref[:, blk * LANES:(blk + 1) * LANES] * u_buf[1 - slot, blk]
                             for blk in range(n_ublk)], axis=1)
        y = jax.nn.gelu(y)
        gate = jnp.dot(y.astype(BF16), gw_ref[...], preferred_element_type=F32) + gbias_ref[...]
        ob_ref[...] = (y * jax.nn.sigmoid(gate)).astype(ob_ref.dtype)

    opens_sequence = (i + tiles_per_seq - 1) % tiles_per_seq == 0
    carry = [(jnp.where(opens_sequence, 0.0, cre_st[idx]), jnp.where(opens_sequence, 0.0, cim_st[idx]))
             for idx in range(n_slab)]

    xb = x_ref[...].astype(BF16)
    w = SB_WIDTH
    g0 = 3 * w + width
    half = d_model // 2

    def project(c0, n):
        return jnp.dot(xb, w_ref[:, c0:c0 + n], preferred_element_type=F32)

    def gate(out_ref, c0, lo):
        out_ref[:, lo:lo + half] = jax.nn.sigmoid(project(c0 + lo, half)).astype(BF16)

    q_ref[...] = (project(0, w) * q_scale).astype(BF16)
    drive()
    carry = scan(carry)
    k_ref[...] = project(w, w).astype(BF16)
    v_ref[...] = project(2 * w, w).astype(BF16)
    gate(ga_ref, g0, 0)
    readout()
    gate(ga_ref, g0, half)
    gate(gb_ref, g0 + d_model, 0)
    gate(gb_ref, g0 + d_model, half)
    for idx in range(n_slab):
        cre_st[idx] = carry[idx][0]
        cim_st[idx] = carry[idx][1]
    u_new = project(3 * w, width)
    for blk in range(n_ublk):
        u_buf[slot, blk] = u_new[:, blk * LANES:(blk + 1) * LANES]


def _block_diag(blocks, dtype):
    g, r, c = blocks.shape
    wide = blocks.transpose(1, 0, 2).reshape(r, g * c)
    row_g = lax.broadcasted_iota(jnp.int32, (g * r, g * c), 0) // r
    col_g = lax.broadcasted_iota(jnp.int32, (g * r, g * c), 1) // c
    return jnp.where(row_g == col_g, jnp.tile(wide, (g, 1)), 0.0).astype(dtype)


def _in_proj_s5(x2, seq, w_in_bf, lam_re, lam_im, log_dt, b_re, b_im, c_re, c_im, d_skip, glu_w, glu_b):
    t, d = x2.shape
    width = d_skip.shape[0]
    n_ublk = width // LANES
    k = _s5_constants(lam_re, lam_im, log_dt, b_re, b_im, c_re, c_im)
    n_state = k["a2_re"].size
    n_slab = n_state // (SUBLANES * LANES)
    to_states = lambda m: _block_diag(m.transpose(0, 2, 1), BF16)
    to_y = lambda m: _block_diag(m, BF16)
    a_re = k["a2_re"].reshape(n_slab, SUBLANES, LANES)
    a_im = k["a2_im"].reshape(n_slab, SUBLANES, LANES)
    tm = ROW_TILE
    n_tiles = t // tm
    cur = lambda n: pl.BlockSpec((tm, n), lambda i: (jnp.minimum(i, n_tiles - 1), 0))
    prev = lambda n: pl.BlockSpec((tm, n), lambda i: (jnp.maximum(i - 1, 0), 0))
    sds = lambda n, dt: jax.ShapeDtypeStruct((t, n), dt)
    consts = (w_in_bf, to_states(k["ab_re"]), to_states(k["ab_im"]), to_states(k["bb_re"]), to_states(k["bb_im"]),
              to_y(k["ac_re"]), to_y(k["ac_im"]), to_y(k["a2c_re"]), to_y(k["a2c_im"]),
              to_states(k["cb"]), to_states(k["cab"]), a_re, a_im,
              d_skip.reshape(1, width), glu_w.astype(BF16), glu_b.reshape(1, width))
    slab_buf = pltpu.VMEM((n_slab, (tm // 2) * SUBLANES, LANES), F32)
    carry = pltpu.VMEM((n_slab, SUBLANES, LANES), F32)
    return pl.pallas_call(
        functools.partial(_in_proj_s5_kernel, q_scale=LOG2_E * SB_HEAD_DIM ** -0.5,
                          tiles_per_seq=seq // tm),
        grid=(n_tiles + 1,),
        in_specs=[cur(d)] + [_const_spec(c.shape) for c in consts],
        out_specs=[cur(SB_WIDTH), cur(SB_WIDTH), cur(SB_WIDTH), cur(d), cur(d), prev(width)],
        out_shape=[sds(SB_WIDTH, BF16), sds(SB_WIDTH, BF16), sds(SB_WIDTH, BF16),
                   sds(d, BF16), sds(d, BF16), sds(width, BF16)],
        scratch_shapes=[pltpu.VMEM((2, n_ublk, tm, LANES), F32), pltpu.VMEM((n_ublk, tm, LANES), F32),
                        carry, carry, slab_buf, slab_buf],
        compiler_params=pltpu.CompilerParams(dimension_semantics=("arbitrary",),
                                             vmem_limit_bytes=VMEM_LIMIT),
        name="in_proj_s5",
    )(x2, *consts)


def _suffix_sum_matrix():
    j = lax.broadcasted_iota(jnp.int32, (SUFFIX_UNIT, SUFFIX_UNIT), 0)
    s = lax.broadcasted_iota(jnp.int32, (SUFFIX_UNIT, SUFFIX_UNIT), 1)
    return jnp.where(j > s, 1.0, 0.0).astype(BF16)


def _attn_kernel(q_ref, k_ref, v_ref, m_ref, vis_ref, o_ref):
    tq = q_ref.shape[0]
    blk = SUFFIX_UNIT
    n_blk = tq // blk
    n_pairs = q_ref.shape[1] // LANES
    i = pl.program_id(2)
    lane = lax.broadcasted_iota(jnp.int32, (1, LANES), 1)
    m = m_ref[...]

    def stacked_queries(hp):
        q2 = q_ref[:, hp * LANES:(hp + 1) * LANES]
        zero = jnp.zeros_like(q2)
        heads = (jnp.where(lane < SB_HEAD_DIM, q2, zero), jnp.where(lane >= SB_HEAD_DIM, q2, zero))
        return jnp.concatenate([h[b * blk:(b + 1) * blk] for b in range(n_blk) for h in heads], axis=0)

    def unit(hp, qs, start, acc, ls, r0, mask=None, gain=None):
        def masked(x):
            if gain is not None:
                x = x * gain
            if mask is None:
                return x
            top = x[:mask.shape[0]] * mask
            return top if mask.shape[0] == x.shape[0] else jnp.concatenate([top, x[mask.shape[0]:]], axis=0)

        ks = k_ref[pl.ds(start, blk), hp * LANES:(hp + 1) * LANES]
        vs = v_ref[pl.ds(start, blk), hp * LANES:(hp + 1) * LANES]
        z = lax.dot_general(qs[r0:], ks, (((1,), (1,)), ((), ())), preferred_element_type=F32)
        sp = jnp.maximum(z, jnp.log(1.0 + jnp.exp2(jnp.minimum(z, SP_LINEAR_FROM))) * LOG2_E)
        d = z - sp
        hi = masked(sp).astype(BF16)
        suffix = jnp.dot(hi, m, preferred_element_type=F32)
        ls_in = ls[r0:]
        w = masked(jnp.exp2(d - suffix + jnp.concatenate([ls_in, ls_in], axis=1)))
        acc_out = acc[r0:] + jnp.dot(w.astype(BF16), vs, preferred_element_type=F32)
        ls_out = ls_in - (suffix[:, :1] + hi[:, :1].astype(F32))
        if r0 == 0:
            return acc_out, ls_out
        return (jnp.concatenate([acc[:r0], acc_out], axis=0),
                jnp.concatenate([ls[:r0], ls_out], axis=0))

    zeros = jnp.zeros((2 * tq, LANES), F32)
    n_older = i * n_blk
    prev_start = pl.multiple_of(jnp.maximum(i * tq - blk, 0), blk)
    has_prev = (i > 0).astype(F32)

    qs_all, state = [], []
    for hp in range(n_pairs):
        qs = stacked_queries(hp)
        acc, ls = zeros, zeros
        for j in reversed(range(n_blk)):
            acc, ls = unit(hp, qs, pl.multiple_of(i * tq + j * blk, blk), acc, ls, 2 * j * blk,
                           mask=vis_ref[...])
        acc, ls = unit(hp, qs, prev_start, acc, ls, 0, gain=has_prev)
        qs_all.append(qs)
        state.append((acc, ls))

    for hp in range(n_pairs):
        qs = qs_all[hp]

        def more(c):
            return jnp.logical_and(c[0] < n_older, jnp.max(c[2]) > LS_FLOOR)

        def older_unit(c, hp=hp, qs=qs):
            n, acc, ls = c
            acc, ls = unit(hp, qs, pl.multiple_of((n_older - 1 - n) * blk, blk), acc, ls, 0)
            return n + 1, acc, ls

        _, acc, ls = lax.while_loop(more, older_unit, (jnp.minimum(i, 1),) + state[hp])
        per_head = [jnp.concatenate([acc[(2 * b + h) * blk:(2 * b + h + 1) * blk]
                                     for b in range(n_blk)], axis=0) for h in range(2)]
        o_ref[:, hp * LANES:(hp + 1) * LANES] = jnp.where(
            lane < SB_HEAD_DIM, per_head[0], per_head[1]).astype(o_ref.dtype)


def _causal_visibility():
    r = lax.broadcasted_iota(jnp.int32, (2 * SUFFIX_UNIT, SUFFIX_UNIT), 0)
    col = lax.broadcasted_iota(jnp.int32, (2 * SUFFIX_UNIT, SUFFIX_UNIT), 1)
    return (col < r % SUFFIX_UNIT).astype(F32)


def _attention(q, k, v, bsz, seq):
    t = q.shape[0]
    tq = ATTN_TQ
    nq = seq // tq
    cols = ATTN_PAIRS * LANES
    n_blk = SB_WIDTH // cols
    qspec = pl.BlockSpec((tq, cols), lambda b, h, i: (b * nq + i, h))
    kvspec = pl.BlockSpec((seq, cols), lambda b, h, i: (b, h))
    return pl.pallas_call(
        _attn_kernel,
        grid=(bsz, n_blk, nq),
        in_specs=[qspec, kvspec, kvspec, _const_spec((SUFFIX_UNIT, SUFFIX_UNIT)),
                  _const_spec((2 * SUFFIX_UNIT, SUFFIX_UNIT))],
        out_specs=qspec,
        out_shape=jax.ShapeDtypeStruct((t, SB_WIDTH), BF16),
        compiler_params=pltpu.CompilerParams(
            dimension_semantics=("parallel", "parallel", "arbitrary"),
            vmem_limit_bytes=VMEM_LIMIT),
        name="stickbreak_attention",
    )(q, k, v, _suffix_sum_matrix(), _causal_visibility())


def _merge_ln1_kernel(x_ref, oa_ref, ob_ref, ga_ref, gb_ref, wa_ref, wb_ref, wo_ref,
                      g_ref, b_ref, h_ref, *, alpha):
    sub = x_ref.shape[0] // 2
    for r0 in (0, sub):
        rows = slice(r0, r0 + sub)
        pa = jnp.dot(oa_ref[rows, :], wa_ref[...], preferred_element_type=F32)
        pb = jnp.dot(ob_ref[rows, :], wb_ref[...], preferred_element_type=F32)
        merged = ga_ref[rows, :].astype(F32) * pa + gb_ref[rows, :].astype(F32) * pb
        r = alpha * x_ref[rows, :] + jnp.dot(merged.astype(BF16), wo_ref[...],
                                             preferred_element_type=F32)
        h_ref[rows, :] = _layer_norm(r, g_ref[...], b_ref[...])


def _merge_ln1(x2, o_a, o_b, sg_a, sg_b, w_a, w_b, w_out, ln_g, ln_b, alpha):
    t, d = x2.shape
    tm = ROW_TILE
    row = lambda n: pl.BlockSpec((tm, n), lambda i: (i, 0))
    return pl.pallas_call(
        functools.partial(_merge_ln1_kernel, alpha=alpha),
        grid=(t // tm,),
        in_specs=[row(d), row(o_a.shape[1]), row(o_b.shape[1]), row(d), row(d),
                  _const_spec(w_a.shape), _const_spec(w_b.shape), _const_spec(w_out.shape),
                  _const_spec((1, d)), _const_spec((1, d))],
        out_specs=row(d),
        out_shape=jax.ShapeDtypeStruct((t, d), F32),
        compiler_params=pltpu.CompilerParams(dimension_semantics=("parallel",),
                                             vmem_limit_bytes=VMEM_LIMIT),
        name="merge_ln1",
    )(x2, o_a, o_b, sg_a, sg_b, w_a, w_b, w_out, ln_g.reshape(1, d), ln_b.reshape(1, d))


def _ffn_ple_ln2_kernel(h_ref, p_ref, wg_ref, wu_ref, wd_ref, wpg_ref, wpp_ref,
                        g_ref, b_ref, o_ref, *, alpha, ff_chunk):
    h = h_ref[...]
    hb = h.astype(BF16)
    d_ff = wg_ref.shape[1]
    ffn = jnp.zeros(h.shape, F32)
    for c0 in range(0, d_ff, ff_chunk):
        gate = jnp.dot(hb, wg_ref[:, c0:c0 + ff_chunk], preferred_element_type=F32)
        up = jnp.dot(hb, wu_ref[:, c0:c0 + ff_chunk], preferred_element_type=F32)
        act = (jax.nn.silu(gate) * up).astype(BF16)
        ffn = ffn + jnp.dot(act, wd_ref[c0:c0 + ff_chunk, :], preferred_element_type=F32)
    ple_gate = jax.nn.sigmoid(jnp.dot(hb, wpg_ref[...], preferred_element_type=F32))
    ple = ple_gate * jnp.dot(p_ref[...].astype(BF16), wpp_ref[...], preferred_element_type=F32)
    o_ref[...] = _layer_norm(alpha * h + ffn + ple, g_ref[...], b_ref[...])


def _ffn_ple_ln2(h1, p2, w_gate, w_up, w_down, w_pg, w_pp, ln_g, ln_b, alpha):
    t, d = h1.shape
    tm = ROW_TILE
    d_ff = w_gate.shape[1]
    ff_chunk = 2 * LANES if d_ff % (2 * LANES) == 0 else d_ff
    row = lambda n: pl.BlockSpec((tm, n), lambda i: (i, 0))
    return pl.pallas_call(
        functools.partial(_ffn_ple_ln2_kernel, alpha=alpha, ff_chunk=ff_chunk),
        grid=(t // tm,),
        in_specs=[row(d), row(p2.shape[1]),
                  _const_spec(w_gate.shape), _const_spec(w_up.shape), _const_spec(w_down.shape),
                  _const_spec(w_pg.shape), _const_spec(w_pp.shape),
                  _const_spec((1, d)), _const_spec((1, d))],
        out_specs=row(d),
        out_shape=jax.ShapeDtypeStruct((t, d), F32),
        compiler_params=pltpu.CompilerParams(dimension_semantics=("parallel",),
                                             vmem_limit_bytes=VMEM_LIMIT),
        name="ffn_ple_ln2",
    )(h1, p2, w_gate, w_up, w_down, w_pg, w_pp, ln_g.reshape(1, d), ln_b.reshape(1, d))


def kernel(x, p, w_in, ssm_lambda_re, ssm_lambda_im, ssm_log_dt, ssm_b_re, ssm_b_im, ssm_c_re, ssm_c_im, ssm_d, ssm_glu_w, ssm_glu_b, w_branch_a, w_branch_b, w_out, ln1_g, ln1_b, ffn_w_gate, ffn_w_up, ffn_w_down, ple_w_gate, ple_w_proj, ln2_g, ln2_b):
    bsz, seq, d = x.shape
    depth = w_in.shape[0]
    alpha = (2 * depth) ** 0.25
    t = bsz * seq
    h = x.reshape(t, d)
    for i in range(depth):
        q, k, v, sg_a, sg_b, o_b = _in_proj_s5(
            h, seq, w_in[i].astype(BF16), ssm_lambda_re[i], ssm_lambda_im[i], ssm_log_dt[i],
            ssm_b_re[i], ssm_b_im[i], ssm_c_re[i], ssm_c_im[i], ssm_d[i], ssm_glu_w[i], ssm_glu_b[i])
        o_a = _attention(q, k, v, bsz, seq)
        h = _merge_ln1(h, o_a, o_b, sg_a, sg_b, w_branch_a[i].astype(BF16), w_branch_b[i].astype(BF16),
                       w_out[i].astype(BF16), ln1_g[i], ln1_b[i], alpha)
        h = _ffn_ple_ln2(h, p[i].reshape(t, -1), ffn_w_gate[i].astype(BF16), ffn_w_up[i].astype(BF16),
                         ffn_w_down[i].astype(BF16), ple_w_gate[i].astype(BF16),
                         ple_w_proj[i].astype(BF16), ln2_g[i], ln2_b[i], alpha)
    return h.reshape(bsz, seq, d)
```

```python
import functools

import jax
import jax.numpy as jnp
from jax import lax
from jax.experimental import pallas as pl
from jax.experimental.pallas import tpu as pltpu

F32 = jnp.float32
BF16 = jnp.bfloat16

LN_EPS = 1e-5
SB_HEADS = 8
SB_HEAD_DIM = 64
SB_WIDTH = SB_HEADS * SB_HEAD_DIM
SSM_GROUP = 16
SSM_STATE = 64

LANES = 128
SUBLANES = 8
SUFFIX_UNIT = 2 * LANES
LOG2_E = 1.4426950408889634
SP_LINEAR_FROM = 126.0
LS_FLOOR = -200.0
VMEM_LIMIT = 56 * 1024 * 1024

ROW_TILE = 512
ATTN_TQ = 512
ATTN_PAIRS = 2


def _const_spec(shape):
    nd = len(shape)
    return pl.BlockSpec(shape, lambda *_: (0,) * nd, pipeline_mode=pl.Buffered(1))


def _layer_norm(r, g, b):
    mu = jnp.mean(r, axis=-1, keepdims=True)
    d = r - mu
    var = jnp.mean(d * d, axis=-1, keepdims=True)
    return d * lax.rsqrt(var + LN_EPS) * g + b


def _s5_discretize_kernel(lre_ref, lim_ref, ldt_ref, bre_ref, bim_ref, cre_ref, cim_ref,
                          a2re_ref, a2im_ref, bbre_ref, bbim_ref, abre_ref, abim_ref,
                          acre_ref, acim_ref, a2cre_ref, a2cim_ref):
    lam_re = lre_ref[...]
    lam_im = lim_ref[...]
    dt = jnp.exp(ldt_ref[...])
    mag = jnp.exp(lam_re * dt)
    a_re = mag * jnp.cos(lam_im * dt)
    a_im = mag * jnp.sin(lam_im * dt)
    den = lam_re * lam_re + lam_im * lam_im
    n_re = a_re - 1.0
    coef_re = (n_re * lam_re + a_im * lam_im) / den
    coef_im = (a_im * lam_re - n_re * lam_im) / den
    b_re = bre_ref[...]
    b_im = bim_ref[...]
    bb_re = coef_re * b_re - coef_im * b_im
    bb_im = coef_re * b_im + coef_im * b_re
    a2_re = a_re * a_re - a_im * a_im
    a2_im = 2.0 * a_re * a_im
    c_re = cre_ref[...]
    c_im = cim_ref[...]
    a2re_ref[...] = a2_re
    a2im_ref[...] = a2_im
    bbre_ref[...] = bb_re
    bbim_ref[...] = bb_im
    abre_ref[...] = a_re * bb_re - a_im * bb_im
    abim_ref[...] = a_re * bb_im + a_im * bb_re
    acre_ref[...] = a_re * c_re - a_im * c_im
    acim_ref[...] = a_re * c_im + a_im * c_re
    a2cre_ref[...] = a2_re * c_re - a2_im * c_im
    a2cim_ref[...] = a2_re * c_im + a2_im * c_re


def _s5_pair_terms_kernel(cre_ref, cim_ref, bbre_ref, bbim_ref, abre_ref, abim_ref, cb_ref, cab_ref):
    def contract(c_ref, x_ref):
        return lax.dot_general(c_ref[...], x_ref[...], (((2,), (1,)), ((0,), (0,))),
                               preferred_element_type=F32)

    cb_ref[...] = contract(cre_ref, bbre_ref) - contract(cim_ref, bbim_ref)
    cab_ref[...] = contract(cre_ref, abre_ref) - contract(cim_ref, abim_ref)


def _s5_constants(lam_re, lam_im, log_dt, b_re, b_im, c_re, c_im):
    g, p = lam_re.shape
    c = b_re.shape[-1]
    rep = lambda a: jnp.repeat(a, c, axis=1)
    flat = lambda a: a.reshape(g, p * c)
    unflat = lambda a: a.reshape(g, p, c)
    ldt = jnp.broadcast_to(log_dt[:, None], (g, p * c))
    ct_re, ct_im = c_re.transpose(0, 2, 1), c_im.transpose(0, 2, 1)
    shp = jax.ShapeDtypeStruct((g, p * c), F32)
    a2_re, a2_im, bb_re, bb_im, ab_re, ab_im, ac_re, ac_im, a2c_re, a2c_im = pl.pallas_call(
        _s5_discretize_kernel, out_shape=(shp,) * 10, name="s5_discretize",
    )(rep(lam_re), rep(lam_im), ldt, flat(b_re), flat(b_im), flat(ct_re), flat(ct_im))
    pair = jax.ShapeDtypeStruct((g, c, c), F32)
    cb, cab = pl.pallas_call(
        _s5_pair_terms_kernel, out_shape=(pair, pair), name="s5_pair_terms",
    )(c_re, c_im, unflat(bb_re), unflat(bb_im), unflat(ab_re), unflat(ab_im))
    return dict(a2_re=a2_re[:, ::c], a2_im=a2_im[:, ::c],
                bb_re=unflat(bb_re), bb_im=unflat(bb_im), ab_re=unflat(ab_re), ab_im=unflat(ab_im),
                ac_re=unflat(ac_re), ac_im=unflat(ac_im), a2c_re=unflat(a2c_re), a2c_im=unflat(a2c_im),
                cb=cb, cab=cab)


def _in_proj_s5_kernel(x_ref, w_ref, abre_ref, abim_ref, bbre_ref, bbim_ref,
                       acre_ref, acim_ref, a2cre_ref, a2cim_ref, cb_ref, cab_ref, are_ref, aim_ref,
                       d_ref, gw_ref, gbias_ref,
                       q_ref, k_ref, v_ref, ga_ref, gb_ref, ob_ref,
                       u_buf, y_buf, cre_st, cim_st, sre_buf, sim_buf, *, q_scale, tiles_per_seq):
    i = pl.program_id(0)
    tm, d_model = x_ref.shape
    n_ublk = u_buf.shape[1]
    width = n_ublk * LANES
    n_pair = tm // 2
    n_chunk = bbre_ref.shape[1] // LANES
    n_slab = n_chunk // SUBLANES
    ch_per_ublk = n_chunk // n_ublk

    @pl.when(i == 0)
    def _():
        u_buf[...] = jnp.zeros_like(u_buf)
        cre_st[...] = jnp.zeros_like(cre_st)
        cim_st[...] = jnp.zeros_like(cim_st)

    slot = i % 2

    def u_pairs(blk):
        even = u_buf[1 - slot, blk, pl.ds(0, n_pair, stride=2), :]
        odd = u_buf[1 - slot, blk, pl.ds(1, n_pair, stride=2), :]
        return jnp.concatenate([even, odd], axis=1).astype(BF16)

    def slab_rows(c):
        return pl.ds(c % SUBLANES, n_pair, stride=SUBLANES)

    def drive():
        for blk in range(n_ublk):
            up = u_pairs(blk)
            rows = slice(blk * LANES, (blk + 1) * LANES)
            cols = slice(blk * ch_per_ublk * LANES, (blk + 1) * ch_per_ublk * LANES)
            for ab_ref, bb_ref, buf in ((abre_ref, bbre_ref, sre_buf), (abim_ref, bbim_ref, sim_buf)):
                w2 = jnp.concatenate([ab_ref[rows, cols], bb_ref[rows, cols]], axis=0)
                val = jnp.dot(up, w2, preferred_element_type=F32)
                for e in range(ch_per_ublk):
                    c = blk * ch_per_ublk + e
                    buf[c // SUBLANES, slab_rows(c), :] = val[:, e * LANES:(e + 1) * LANES]

    a_re = [are_ref[k] for k in range(n_slab)]
    a_im = [aim_ref[k] for k in range(n_slab)]

    def scan(carry):
        for t in range(n_pair):
            tok = pl.ds(t * SUBLANES, SUBLANES)
            new = []
            for idx in range(n_slab):
                s_re, s_im = carry[idx]
                n_re = a_re[idx] * s_re - a_im[idx] * s_im + sre_buf[idx, tok, :]
                n_im = a_re[idx] * s_im + a_im[idx] * s_re + sim_buf[idx, tok, :]
                sre_buf[idx, tok, :] = s_re
                sim_buf[idx, tok, :] = s_im
                new.append((n_re, n_im))
            carry = new
        return carry

    def readout():
        for blk in range(n_ublk):
            cols = slice(blk * LANES, (blk + 1) * LANES)
            cb = cb_ref[cols, cols]
            intra = jnp.concatenate([jnp.concatenate([cb, cab_ref[cols, cols]], axis=1),
                                     jnp.concatenate([jnp.zeros_like(cb), cb], axis=1)], axis=0)
            acc = jnp.dot(u_pairs(blk), intra, preferred_element_type=F32)
            for half in range(ch_per_ublk // 2):
                c0 = blk * ch_per_ublk + 2 * half
                rows = slice(c0 * LANES, (c0 + 2) * LANES)

                def chunk_pair(buf):
                    parts = [buf[(c0 + e) // SUBLANES, slab_rows(c0 + e), :] for e in range(2)]
                    return jnp.concatenate(parts, axis=1).astype(BF16)

                r_re = jnp.concatenate([acre_ref[rows, cols], a2cre_ref[rows, cols]], axis=1)
                r_im = jnp.concatenate([acim_ref[rows, cols], a2cim_ref[rows, cols]], axis=1)
                acc = acc + jnp.dot(chunk_pair(sre_buf), r_re, preferred_element_type=F32)
                acc = acc - jnp.dot(chunk_pair(sim_buf), r_im, preferred_element_type=F32)
            y_buf[blk, pl.ds(0, n_pair, stride=2), :] = acc[:, :LANES]
            y_buf[blk, pl.ds(1, n_pair, stride=2), :] = acc[:, LANES:]
        y = jnp.concatenate([y_buf[blk] + d_ref[:, blk * LANES:(blk + 1) * LANES] * u_buf[1 - slot, blk]
                             for blk in range(n_ublk)], axis=1)
        y = jax.nn.gelu(y)
        gate = jnp.dot(y.astype(BF16), gw_ref[...], preferred_element_type=F32) + gbias_ref[...]
        ob_ref[...] = (y * jax.nn.sigmoid(gate)).astype(ob_ref.dtype)

    opens_sequence = (i + tiles_per_seq - 1) % tiles_per_seq == 0
    carry = [(jnp.where(opens_sequence, 0.0, cre_st[idx]), jnp.where(opens_sequence, 0.0, cim_st[idx]))
             for idx in range(n_slab)]

    xb = x_ref[...].astype(BF16)
    w = SB_WIDTH
    g0 = 3 * w + width
    half = d_model // 2

    def project(c0, n):
        return jnp.dot(xb, w_ref[:, c0:c0 + n], preferred_element_type=F32)

    def gate(out_ref, c0, lo):
        out_ref[:, lo:lo + half] = jax.nn.sigmoid(project(c0 + lo, half)).astype(BF16)

    q_ref[...] = (project(0, w) * q_scale).astype(BF16)
    drive()
    carry = scan(carry)
    k_ref[...] = project(w, w).astype(BF16)
    v_ref[...] = project(2 * w, w).astype(BF16)
    gate(ga_ref, g0, 0)
    readout()
    gate(ga_ref, g0, half)
    gate(gb_ref, g0 + d_model, 0)
    gate(gb_ref, g0 + d_model, half)
    for idx in range(n_slab):
        cre_st[idx] = carry[idx][0]
        cim_st[idx] = carry[idx][1]
    u_new = project(3 * w, width)
    for blk in range(n_ublk):
        u_buf[slot, blk] = u_new[:, blk * LANES:(blk + 1) * LANES]


def _block_diag(blocks, dtype):
    g, r, c = blocks.shape
    wide = blocks.transpose(1, 0, 2).reshape(r, g * c)
    row_g = lax.broadcasted_iota(jnp.int32, (g * r, g * c), 0) // r
    col_g = lax.broadcasted_iota(jnp.int32, (g * r, g * c), 1) // c
    return jnp.where(row_g == col_g, jnp.tile(wide, (g, 1)), 0.0).astype(dtype)


def _in_proj_s5(x2, seq, w_in_bf, lam_re, lam_im, log_dt, b_re, b_im, c_re, c_im, d_skip, glu_w, glu_b):
    t, d = x2.shape
    width = d_skip.shape[0]
    n_ublk = width // LANES
    k = _s5_constants(lam_re, lam_im, log_dt, b_re, b_im, c_re, c_im)
    n_state = k["a2_re"].size
    n_slab = n_state // (SUBLANES * LANES)
    to_states = lambda m: _block_diag(m.transpose(0, 2, 1), BF16)
    to_y = lambda m: _block_diag(m, BF16)
    a_re = k["a2_re"].reshape(n_slab, SUBLANES, LANES)
    a_im = k["a2_im"].reshape(n_slab, SUBLANES, LANES)
    tm = ROW_TILE
    n_tiles = t // tm
    cur = lambda n: pl.BlockSpec((tm, n), lambda i: (jnp.minimum(i, n_tiles - 1), 0))
    prev = lambda n: pl.BlockSpec((tm, n), lambda i: (jnp.maximum(i - 1, 0), 0))
    sds = lambda n, dt: jax.ShapeDtypeStruct((t, n), dt)
    consts = (w_in_bf, to_states(k["ab_re"]), to_states(k["ab_im"]), to_states(k["bb_re"]), to_states(k["bb_im"]),
              to_y(k["ac_re"]), to_y(k["ac_im"]), to_y(k["a2c_re"]), to_y(k["a2c_im"]),
              to_states(k["cb"]), to_states(k["cab"]), a_re, a_im,
              d_skip.reshape(1, width), glu_w.astype(BF16), glu_b.reshape(1, width))
    slab_buf = pltpu.VMEM((n_slab, (tm // 2) * SUBLANES, LANES), F32)
    carry = pltpu.VMEM((n_slab, SUBLANES, LANES), F32)
    return pl.pallas_call(
        functools.partial(_in_proj_s5_kernel, q_scale=LOG2_E * SB_HEAD_DIM ** -0.5,
                          tiles_per_seq=seq // tm),
        grid=(n_tiles + 1,),
        in_specs=[cur(d)] + [_const_spec(c.shape) for c in consts],
        out_specs=[cur(SB_WIDTH), cur(SB_WIDTH), cur(SB_WIDTH), cur(d), cur(d), prev(width)],
        out_shape=[sds(SB_WIDTH, BF16), sds(SB_WIDTH, BF16), sds(SB_WIDTH, BF16),
                   sds(d, BF16), sds(d, BF16), sds(width, BF16)],
        scratch_shapes=[pltpu.VMEM((2, n_ublk, tm, LANES), F32), pltpu.VMEM((n_ublk, tm, LANES), F32),
                        carry, carry, slab_buf, slab_buf],
        compiler_params=pltpu.CompilerParams(dimension_semantics=("arbitrary",),
                                             vmem_limit_bytes=VMEM_LIMIT),
        name="in_proj_s5",
    )(x2, *consts)


def _suffix_sum_matrix():
    j = lax.broadcasted_iota(jnp.int32, (SUFFIX_UNIT, SUFFIX_UNIT), 0)
    s = lax.broadcasted_iota(jnp.int32, (SUFFIX_UNIT, SUFFIX_UNIT), 1)
    return jnp.where(j > s, 1.0, 0.0).astype(BF16)


def _attn_kernel(q_ref, k_ref, v_ref, m_ref, vis_ref, o_ref):
    tq = q_ref.shape[0]
    blk = SUFFIX_UNIT
    n_blk = tq // blk
    n_pairs = q_ref.shape[1] // LANES
    i = pl.program_id(2)
    lane = lax.broadcasted_iota(jnp.int32, (1, LANES), 1)
    m = m_ref[...]

    def stacked_queries(hp):
        q2 = q_ref[:, hp * LANES:(hp + 1) * LANES]
        zero = jnp.zeros_like(q2)
        heads = (jnp.where(lane < SB_HEAD_DIM, q2, zero), jnp.where(lane >= SB_HEAD_DIM, q2, zero))
        return jnp.concatenate([h[b * blk:(b + 1) * blk] for b in range(n_blk) for h in heads], axis=0)

    def unit(hp, qs, start, acc, ls, r0, mask=None, gain=None):
        def masked(x):
            if gain is not None:
                x = x * gain
            if mask is None:
                return x
            top = x[:mask.shape[0]] * mask
            return top if mask.shape[0] == x.shape[0] else jnp.concatenate([top, x[mask.shape[0]:]], axis=0)

        ks = k_ref[pl.ds(start, blk), hp * LANES:(hp + 1) * LANES]
        vs = v_ref[pl.ds(start, blk), hp * LANES:(hp + 1) * LANES]
        z = lax.dot_general(qs[r0:], ks, (((1,), (1,)), ((), ())), preferred_element_type=F32)
        sp = jnp.maximum(z, jnp.log(1.0 + jnp.exp2(jnp.minimum(z, SP_LINEAR_FROM))) * LOG2_E)
        d = z - sp
        hi = masked(sp).astype(BF16)
        suffix = jnp.dot(hi, m, preferred_element_type=F32)
        ls_in = ls[r0:]
        w = masked(jnp.exp2(d - suffix + jnp.concatenate([ls_in, ls_in], axis=1)))
        acc_out = acc[r0:] + jnp.dot(w.astype(BF16), vs, preferred_element_type=F32)
        ls_out = ls_in - (suffix[:, :1] + hi[:, :1].astype(F32))
        if r0 == 0:
            return acc_out, ls_out
        return (jnp.concatenate([acc[:r0], acc_out], axis=0),
                jnp.concatenate([ls[:r0], ls_out], axis=0))

    zeros = jnp.zeros((2 * tq, LANES), F32)
    n_older = i * n_blk
    prev_start = pl.multiple_of(jnp.maximum(i * tq - blk, 0), blk)
    has_prev = (i > 0).astype(F32)

    qs_all, state = [], []
    for hp in range(n_pairs):
        qs = stacked_queries(hp)
        acc, ls = zeros, zeros
        for j in reversed(range(n_blk)):
            acc, ls = unit(hp, qs, pl.multiple_of(i * tq + j * blk, blk), acc, ls, 2 * j * blk,
                           mask=vis_ref[...])
        acc, ls = unit(hp, qs, prev_start, acc, ls, 0, gain=has_prev)
        qs_all.append(qs)
        state.append((acc, ls))

    for hp in range(n_pairs):
        qs = qs_all[hp]

        def more(c):
            return jnp.logical_and(c[0] < n_older, jnp.max(c[2]) > LS_FLOOR)

        def older_unit(c, hp=hp, qs=qs):
            n, acc, ls = c
            acc, ls = unit(hp, qs, pl.multiple_of((n_older - 1 - n) * blk, blk), acc, ls, 0)
            return n + 1, acc, ls

        _, acc, ls = lax.while_loop(more, older_unit, (jnp.minimum(i, 1),) + state[hp])
        per_head = [jnp.concatenate([acc[(2 * b + h) * blk:(2 * b + h + 1) * blk]
                                     for b in range(n_blk)], axis=0) for h in range(2)]
        o_ref[:, hp * LANES:(hp + 1) * LANES] = jnp.where(
            lane < SB_HEAD_DIM, per_head[0], per_head[1]).astype(o_ref.dtype)


def _causal_visibility():
    r = lax.broadcasted_iota(jnp.int32, (2 * SUFFIX_UNIT, SUFFIX_UNIT), 0)
    col = lax.broadcasted_iota(jnp.int32, (2 * SUFFIX_UNIT, SUFFIX_UNIT), 1)
    return (col < r % SUFFIX_UNIT).astype(F32)


def _attention(q, k, v, bsz, seq):
    t = q.shape[0]
    tq = ATTN_TQ
    nq = seq // tq
    cols = ATTN_PAIRS * LANES
    n_blk = SB_WIDTH // cols
    qspec = pl.BlockSpec((tq, cols), lambda b, h, i: (b * nq + i, h))
    kvspec = pl.BlockSpec((seq, cols), lambda b, h, i: (b, h))
    return pl.pallas_call(
        _attn_kernel,
        grid=(bsz, n_blk, nq),
        in_specs=[qspec, kvspec, kvspec, _const_spec((SUFFIX_UNIT, SUFFIX_UNIT)),
                  _const_spec((2 * SUFFIX_UNIT, SUFFIX_UNIT))],
        out_specs=qspec,
        out_shape=jax.ShapeDtypeStruct((t, SB_WIDTH), BF16),
        compiler_params=pltpu.CompilerParams(
            dimension_semantics=("parallel", "parallel", "arbitrary"),
            vmem_limit_bytes=VMEM_LIMIT),
        name="stickbreak_attention",
    )(q, k, v, _suffix_sum_matrix(), _causal_visibility())


def _merge_ln1_kernel(x_ref, oa_ref, ob_ref, ga_ref, gb_ref, wa_ref, wb_ref, wo_ref,
                      g_ref, b_ref, h_ref, *, alpha):
    sub = x_ref.shape[0] // 2
    for r0 in (0, sub):
        rows = slice(r0, r0 + sub)
        pa = jnp.dot(oa_ref[rows, :], wa_ref[...], preferred_element_type=F32)
        pb = jnp.dot(ob_ref[rows, :], wb_ref[...], preferred_element_type=F32)
        merged = ga_ref[rows, :].astype(F32) * pa + gb_ref[rows, :].astype(F32) * pb
        r = alpha * x_ref[rows, :] + jnp.dot(merged.astype(BF16), wo_ref[...],
                                             preferred_element_type=F32)
        h_ref[rows, :] = _layer_norm(r, g_ref[...], b_ref[...])


def _merge_ln1(x2, o_a, o_b, sg_a, sg_b, w_a, w_b, w_out, ln_g, ln_b, alpha):
    t, d = x2.shape
    tm = ROW_TILE
    row = lambda n: pl.BlockSpec((tm, n), lambda i: (i, 0))
    return pl.pallas_call(
        functools.partial(_merge_ln1_kernel, alpha=alpha),
        grid=(t // tm,),
        in_specs=[row(d), row(o_a.shape[1]), row(o_b.shape[1]), row(d), row(d),
                  _const_spec(w_a.shape), _const_spec(w_b.shape), _const_spec(w_out.shape),
                  _const_spec((1, d)), _const_spec((1, d))],
        out_specs=row(d),
        out_shape=jax.ShapeDtypeStruct((t, d), F32),
        compiler_params=pltpu.CompilerParams(dimension_semantics=("parallel",),
                                             vmem_limit_bytes=VMEM_LIMIT),
        name="merge_ln1",
    )(x2, o_a, o_b, sg_a, sg_b, w_a, w_b, w_out, ln_g.reshape(1, d), ln_b.reshape(1, d))


def _ffn_ple_ln2_kernel(h_ref, p_ref, wg_ref, wu_ref, wd_ref, wpg_ref, wpp_ref,
                        g_ref, b_ref, o_ref, *, alpha, ff_chunk):
    h = h_ref[...]
    hb = h.astype(BF16)
    d_ff = wg_ref.shape[1]
    ffn = jnp.zeros(h.shape, F32)
    for c0 in range(0, d_ff, ff_chunk):
        gate = jnp.dot(hb, wg_ref[:, c0:c0 + ff_chunk], preferred_element_type=F32)
        up = jnp.dot(hb, wu_ref[:, c0:c0 + ff_chunk], preferred_element_type=F32)
        act = (jax.nn.silu(gate) * up).astype(BF16)
        ffn = ffn + jnp.dot(act, wd_ref[c0:c0 + ff_chunk, :], preferred_element_type=F32)
    ple_gate = jax.nn.sigmoid(jnp.dot(hb, wpg_ref[...], preferred_element_type=F32))
    ple = ple_gate * jnp.dot(p_ref[...].astype(BF16), wpp_ref[...], preferred_element_type=F32)
    o_ref[...] = _layer_norm(alpha * h + ffn + ple, g_ref[...], b_ref[...])


def _ffn_ple_ln2(h1, p2, w_gate, w_up, w_down, w_pg, w_pp, ln_g, ln_b, alpha):
    t, d = h1.shape
    tm = ROW_TILE
    d_ff = w_gate.shape[1]
    ff_chunk = 2 * LANES if d_ff % (2 * LANES) == 0 else d_ff
    row = lambda n: pl.BlockSpec((tm, n), lambda i: (i, 0))
    return pl.pallas_call(
        functools.partial(_ffn_ple_ln2_kernel, alpha=alpha, ff_chunk=ff_chunk),
        grid=(t // tm,),
        in_specs=[row(d), row(p2.shape[1]),
                  _const_spec(w_gate.shape), _const_spec(w_up.shape), _const_spec(w_down.shape),
                  _const_spec(w_pg.shape), _const_spec(w_pp.shape),
                  _const_spec((1, d)), _const_spec((1, d))],
        out_specs=row(d),
        out_shape=jax.ShapeDtypeStruct((t, d), F32),
        compiler_params=pltpu.CompilerParams(dimension_semantics=("parallel",),
                                             vmem_limit_bytes=VMEM_LIMIT),
        name="ffn_ple_ln2",
    )(h1, p2, w_gate, w_up, w_down, w_pg, w_pp, ln_g.reshape(1, d), ln_b.reshape(1, d))


def kernel(x, p, w_in, ssm_lambda_re, ssm_lambda_im, ssm_log_dt, ssm_b_re, ssm_b_im, ssm_c_re, ssm_c_im, ssm_d, ssm_glu_w, ssm_glu_b, w_branch_a, w_branch_b, w_out, ln1_g, ln1_b, ffn_w_gate, ffn_w_up, ffn_w_down, ple_w_gate, ple_w_proj, ln2_g, ln2_b):
    bsz, seq, d = x.shape
    depth = w_in.shape[0]
    alpha = (2 * depth) ** 0.25
    t = bsz * seq
    h = x.reshape(t, d)
    for i in range(depth):
        q, k, v, sg_a, sg_b, o_b = _in_proj_s5(
            h, seq, w_in[i].astype(BF16), ssm_lambda_re[i], ssm_lambda_im[i], ssm_log_dt[i],
            ssm_b_re[i], ssm_b_im[i], ssm_c_re[i], ssm_c_im[i], ssm_d[i], ssm_glu_w[i], ssm_glu_b[i])
        o_a = _attention(q, k, v, bsz, seq)
        h = _merge_ln1(h, o_a, o_b, sg_a, sg_b, w_branch_a[i].astype(BF16), w_branch_b[i].astype(BF16),
                       w_out[i].astype(BF16), ln1_g[i], ln1_b[i], alpha)
        h = _ffn_ple_ln2(h, p[i].reshape(t, -1), ffn_w_gate[i].astype(BF16), ffn_w_up[i].astype(BF16),
                         ffn_w_down[i].astype(BF16), ple_w_gate[i].astype(BF16),
                         ple_w_proj[i].astype(BF16), ln2_g[i], ln2_b[i], alpha)
    return h.reshape(bsz, seq, d)
```

```python
import functools

import jax
import jax.numpy as jnp
from jax import lax
from jax.experimental import pallas as pl
from jax.experimental.pallas import tpu as pltpu

F32 = jnp.float32
BF16 = jnp.bfloat16

LN_EPS = 1e-5
SB_HEADS = 8
SB_HEAD_DIM = 64
SB_WIDTH = SB_HEADS * SB_HEAD_DIM
SSM_GROUP = 16
SSM_STATE = 64

LANES = 128
SUBLANES = 8
SUFFIX_UNIT = 2 * LANES
LOG2_E = 1.4426950408889634
SP_LINEAR_FROM = 126.0
LS_FLOOR = -200.0
VMEM_LIMIT = 56 * 1024 * 1024

ROW_TILE = 512
ATTN_TQ = 512
ATTN_PAIRS = 2


def _const_spec(shape):
    nd = len(shape)
    return pl.BlockSpec(shape, lambda *_: (0,) * nd, pipeline_mode=pl.Buffered(1))


def _layer_norm(r, g, b):
    mu = jnp.mean(r, axis=-1, keepdims=True)
    d = r - mu
    var = jnp.mean(d * d, axis=-1, keepdims=True)
    return d * lax.rsqrt(var + LN_EPS) * g + b


def _s5_discretize_kernel(lre_ref, lim_ref, ldt_ref, bre_ref, bim_ref, cre_ref, cim_ref,
                          a2re_ref, a2im_ref, bbre_ref, bbim_ref, abre_ref, abim_ref,
                          acre_ref, acim_ref, a2cre_ref, a2cim_ref):
    lam_re = lre_ref[...]
    lam_im = lim_ref[...]
    dt = jnp.exp(ldt_ref[...])
    mag = jnp.exp(lam_re * dt)
    a_re = mag * jnp.cos(lam_im * dt)
    a_im = mag * jnp.sin(lam_im * dt)
    den = lam_re * lam_re + lam_im * lam_im
    n_re = a_re - 1.0
    coef_re = (n_re * lam_re + a_im * lam_im) / den
    coef_im = (a_im * lam_re - n_re * lam_im) / den
    b_re = bre_ref[...]
    b_im = bim_ref[...]
    bb_re = coef_re * b_re - coef_im * b_im
    bb_im = coef_re * b_im + coef_im * b_re
    a2_re = a_re * a_re - a_im * a_im
    a2_im = 2.0 * a_re * a_im
    c_re = cre_ref[...]
    c_im = cim_ref[...]
    a2re_ref[...] = a2_re
    a2im_ref[...] = a2_im
    bbre_ref[...] = bb_re
    bbim_ref[...] = bb_im
    abre_ref[...] = a_re * bb_re - a_im * bb_im
    abim_ref[...] = a_re * bb_im + a_im * bb_re
    acre_ref[...] = a_re * c_re - a_im * c_im
    acim_ref[...] = a_re * c_im + a_im * c_re
    a2cre_ref[...] = a2_re * c_re - a2_im * c_im
    a2cim_ref[...] = a2_re * c_im + a2_im * c_re


def _s5_pair_terms_kernel(cre_ref, cim_ref, bbre_ref, bbim_ref, abre_ref, abim_ref, cb_ref, cab_ref):
    def contract(c_ref, x_ref):
        return lax.dot_general(c_ref[...], x_ref[...], (((2,), (1,)), ((0,), (0,))),
                               preferred_element_type=F32)

    cb_ref[...] = contract(cre_ref, bbre_ref) - contract(cim_ref, bbim_ref)
    cab_ref[...] = contract(cre_ref, abre_ref) - contract(cim_ref, abim_ref)


def _s5_constants(lam_re, lam_im, log_dt, b_re, b_im, c_re, c_im):
    g, p = lam_re.shape
    c = b_re.shape[-1]
    rep = lambda a: jnp.repeat(a, c, axis=1)
    flat = lambda a: a.reshape(g, p * c)
    unflat = lambda a: a.reshape(g, p, c)
    ldt = jnp.broadcast_to(log_dt[:, None], (g, p * c))
    ct_re, ct_im = c_re.transpose(0, 2, 1), c_im.transpose(0, 2, 1)
    shp = jax.ShapeDtypeStruct((g, p * c), F32)
    a2_re, a2_im, bb_re, bb_im, ab_re, ab_im, ac_re, ac_im, a2c_re, a2c_im = pl.pallas_call(
        _s5_discretize_kernel, out_shape=(shp,) * 10, name="s5_discretize",
    )(rep(lam_re), rep(lam_im), ldt, flat(b_re), flat(b_im), flat(ct_re), flat(ct_im))
    pair = jax.ShapeDtypeStruct((g, c, c), F32)
    cb, cab = pl.pallas_call(
        _s5_pair_terms_kernel, out_shape=(pair, pair), name="s5_pair_terms",
    )(c_re, c_im, unflat(bb_re), unflat(bb_im), unflat(ab_re), unflat(ab_im))
    return dict(a2_re=a2_re[:, ::c], a2_im=a2_im[:, ::c],
                bb_re=unflat(bb_re), bb_im=unflat(bb_im), ab_re=unflat(ab_re), ab_im=unflat(ab_im),
                ac_re=unflat(ac_re), ac_im=unflat(ac_im), a2c_re=unflat(a2c_re), a2c_im=unflat(a2c_im),
                cb=cb, cab=cab)


def _in_proj_s5_kernel(x_ref, w_ref, abre_ref, abim_ref, bbre_ref, bbim_ref,
                       acre_ref, acim_ref, a2cre_ref, a2cim_ref, cb_ref, cab_ref, are_ref, aim_ref,
                       d_ref, gw_ref, gbias_ref,
                       q_ref, k_ref, v_ref, ga_ref, gb_ref, ob_ref,
                       u_buf, y_buf, cre_st, cim_st, sre_buf, sim_buf, *, q_scale, tiles_per_seq):
    i = pl.program_id(0)
    tm, d_model = x_ref.shape
    n_ublk = u_buf.shape[1]
    width = n_ublk * LANES
    n_pair = tm // 2
    n_chunk = bbre_ref.shape[1] // LANES
    n_slab = n_chunk // SUBLANES
    ch_per_ublk = n_chunk // n_ublk

    @pl.when(i == 0)
    def _():
        u_buf[...] = jnp.zeros_like(u_buf)
        cre_st[...] = jnp.zeros_like(cre_st)
        cim_st[...] = jnp.zeros_like(cim_st)

    slot = i % 2

    def u_pairs(blk):
        even = u_buf[1 - slot, blk, pl.ds(0, n_pair, stride=2), :]
        odd = u_buf[1 - slot, blk, pl.ds(1, n_pair, stride=2), :]
        return jnp.concatenate([even, odd], axis=1).astype(BF16)

    def slab_rows(c):
        return pl.ds(c % SUBLANES, n_pair, stride=SUBLANES)

    def drive():
        for blk in range(n_ublk):
            up = u_pairs(blk)
            rows = slice(blk * LANES, (blk + 1) * LANES)
            cols = slice(blk * ch_per_ublk * LANES, (blk + 1) * ch_per_ublk * LANES)
            for ab_ref, bb_ref, buf in ((abre_ref, bbre_ref, sre_buf), (abim_ref, bbim_ref, sim_buf)):
                w2 = jnp.concatenate([ab_ref[rows, cols], bb_ref[rows, cols]], axis=0)
                val = jnp.dot(up, w2, preferred_element_type=F32)
                for e in range(ch_per_ublk):
                    c = blk * ch_per_ublk + e
                    buf[c // SUBLANES, slab_rows(c), :] = val[:, e * LANES:(e + 1) * LANES]

    a_re = [are_ref[k] for k in range(n_slab)]
    a_im = [aim_ref[k] for k in range(n_slab)]

    def scan(carry):
        for t in range(n_pair):
            tok = pl.ds(t * SUBLANES, SUBLANES)
            new = []
            for idx in range(n_slab):
                s_re, s_im = carry[idx]
                n_re = a_re[idx] * s_re - a_im[idx] * s_im + sre_buf[idx, tok, :]
                n_im = a_re[idx] * s_im + a_im[idx] * s_re + sim_buf[idx, tok, :]
                sre_buf[idx, tok, :] = s_re
                sim_buf[idx, tok, :] = s_im
                new.append((n_re, n_im))
            carry = new
        return carry

    def readout():
        for blk in range(n_ublk):
            cols = slice(blk * LANES, (blk + 1) * LANES)
            cb = cb_ref[cols, cols]
            intra = jnp.concatenate([jnp.concatenate([cb, cab_ref[cols, cols]], axis=1),
                                     jnp.concatenate([jnp.zeros_like(cb), cb], axis=1)], axis=0)
            acc = jnp.dot(u_pairs(blk), intra, preferred_element_type=F32)
            for half in range(ch_per_ublk // 2):
                c0 = blk * ch_per_ublk + 2 * half
                rows = slice(c0 * LANES, (c0 + 2) * LANES)

                def chunk_pair(buf):
                    parts = [buf[(c0 + e) // SUBLANES, slab_rows(c0 + e), :] for e in range(2)]
                    return jnp.concatenate(parts, axis=1).astype(BF16)

                r_re = jnp.concatenate([acre_ref[rows, cols], a2cre_ref[rows, cols]], axis=1)
                r_im = jnp.concatenate([acim_ref[rows, cols], a2cim_ref[rows, cols]], axis=1)
                acc = acc + jnp.dot(chunk_pair(sre_buf), r_re, preferred_element_type=F32)
                acc = acc - jnp.dot(chunk_pair(sim_buf), r_im, preferred_element_type=F32)
            y_buf[blk, pl.ds(0, n_pair, stride=2), :] = acc[:, :LANES]
            y_buf[blk, pl.ds(1, n_pair, stride=2), :] = acc[:, LANES:]
        y = jnp.concatenate([y_buf[blk] + d_ref[:, blk * LANES:(blk + 1) * LANES] * u_buf[1 - slot, blk]
                             for blk in range(n_ublk)], axis=1)
        y = jax.nn.gelu(y)
        gate = jnp.dot(y.astype(BF16), gw_ref[...], preferred_element_type=F32) + gbias_ref[...]
        ob_ref[...] = (y * jax.nn.sigmoid(gate)).astype(ob_ref.dtype)

    opens_sequence = (i + tiles_per_seq - 1) % tiles_per_seq == 0
    carry = [(jnp.where(opens_sequence, 0.0, cre_st[idx]), jnp.where(opens_sequence, 0.0, cim_st[idx]))
             for idx in range(n_slab)]

    xb = x_ref[...].astype(BF16)
    w = SB_WIDTH
    g0 = 3 * w + width
    half = d_model // 2

    def project(c0, n):
        return jnp.dot(xb, w_ref[:, c0:c0 + n], preferred_element_type=F32)

    def gate(out_ref, c0, lo):
        out_ref[:, lo:lo + half] = jax.nn.sigmoid(project(c0 + lo, half)).astype(BF16)

    q_ref[...] = (project(0, w) * q_scale).astype(BF16)
    drive()
    carry = scan(carry)
    k_ref[...] = project(w, w).astype(BF16)
    v_ref[...] = project(2 * w, w).astype(BF16)
    gate(ga_ref, g0, 0)
    readout()
    gate(ga_ref, g0, half)
    gate(gb_ref, g0 + d_model, 0)
    gate(gb_ref, g0 + d_model, half)
    for idx in range(n_slab):
        cre_st[idx] = carry[idx][0]
        cim_st[idx] = carry[idx][1]
    u_new = project(3 * w, width)
    for blk in range(n_ublk):
        u_buf[slot, blk] = u_new[:, blk * LANES:(blk + 1) * LANES]


def _block_diag(blocks, dtype):
    g, r, c = blocks.shape
    wide = blocks.transpose(1, 0, 2).reshape(r, g * c)
    row_g = lax.broadcasted_iota(jnp.int32, (g * r, g * c), 0) // r
    col_g = lax.broadcasted_iota(jnp.int32, (g * r, g * c), 1) // c
    return jnp.where(row_g == col_g, jnp.tile(wide, (g, 1)), 0.0).astype(dtype)


def _in_proj_s5(x2, seq, w_in_bf, lam_re, lam_im, log_dt, b_re, b_im, c_re, c_im, d_skip, glu_w, glu_b):
    t, d = x2.shape
    width = d_skip.shape[0]
    n_ublk = width // LANES
    k = _s5_constants(lam_re, lam_im, log_dt, b_re, b_im, c_re, c_im)
    n_state = k["a2_re"].size
    n_slab = n_state // (SUBLANES * LANES)
    to_states = lambda m: _block_diag(m.transpose(0, 2, 1), BF16)
    to_y = lambda m: _block_diag(m, BF16)
    a_re = k["a2_re"].reshape(n_slab, SUBLANES, LANES)
    a_im = k["a2_im"].reshape(n_slab, SUBLANES, LANES)
    tm = ROW_TILE
    n_tiles = t // tm
    cur = lambda n: pl.BlockSpec((tm, n), lambda i: (jnp.minimum(i, n_tiles - 1), 0))
    prev = lambda n: pl.BlockSpec((tm, n), lambda i: (jnp.maximum(i - 1, 0), 0))
    sds = lambda n, dt: jax.ShapeDtypeStruct((t, n), dt)
    consts = (w_in_bf, to_states(k["ab_re"]), to_states(k["ab_im"]), to_states(k["bb_re"]), to_states(k["bb_im"]),
              to_y(k["ac_re"]), to_y(k["ac_im"]), to_y(k["a2c_re"]), to_y(k["a2c_im"]),
              to_states(k["cb"]), to_states(k["cab"]), a_re, a_im,
              d_skip.reshape(1, width), glu_w.astype(BF16), glu_b.reshape(1, width))
    slab_buf = pltpu.VMEM((n_slab, (tm // 2) * SUBLANES, LANES), F32)
    carry = pltpu.VMEM((n_slab, SUBLANES, LANES), F32)
    return pl.pallas_call(
        functools.partial(_in_proj_s5_kernel, q_scale=LOG2_E * SB_HEAD_DIM ** -0.5,
                          tiles_per_seq=seq // tm),
        grid=(n_tiles + 1,),
        in_specs=[cur(d)] + [_const_spec(c.shape) for c in consts],
        out_specs=[cur(SB_WIDTH), cur(SB_WIDTH), cur(SB_WIDTH), cur(d), cur(d), prev(width)],
        out_shape=[sds(SB_WIDTH, BF16), sds(SB_WIDTH, BF16), sds(SB_WIDTH, BF16),
                   sds(d, BF16), sds(d, BF16), sds(width, BF16)],
        scratch_shapes=[pltpu.VMEM((2, n_ublk, tm, LANES), F32), pltpu.VMEM((n_ublk, tm, LANES), F32),
                        carry, carry, slab_buf, slab_buf],
        compiler_params=pltpu.CompilerParams(dimension_semantics=("arbitrary",),
                                             vmem_limit_bytes=VMEM_LIMIT),
        name="in_proj_s5",
    )(x2, *consts)


def _suffix_sum_matrix():
    j = lax.broadcasted_iota(jnp.int32, (SUFFIX_UNIT, SUFFIX_UNIT), 0)
    s = lax.broadcasted_iota(jnp.int32, (SUFFIX_UNIT, SUFFIX_UNIT), 1)
    return jnp.where(j > s, 1.0, 0.0).astype(BF16)


def _attn_kernel(q_ref, k_ref, v_ref, m_ref, vis_ref, o_ref):
    tq = q_ref.shape[0]
    blk = SUFFIX_UNIT
    n_blk = tq // blk
    n_pairs = q_ref.shape[1] // LANES
    i = pl.program_id(2)
    lane = lax.broadcasted_iota(jnp.int32, (1, LANES), 1)
    m = m_ref[...]

    def stacked_queries(hp):
        q2 = q_ref[:, hp * LANES:(hp + 1) * LANES]
        zero = jnp.zeros_like(q2)
        heads = (jnp.where(lane < SB_HEAD_DIM, q2, zero), jnp.where(lane >= SB_HEAD_DIM, q2, zero))
        return jnp.concatenate([h[b * blk:(b + 1) * blk] for b in range(n_blk) for h in heads], axis=0)

    def unit(hp, qs, start, acc, ls, r0, mask=None, gain=None):
        def masked(x):
            if gain is not None:
                x = x * gain
            if mask is None:
                return x
            top = x[:mask.shape[0]] * mask
            return top if mask.shape[0] == x.shape[0] else jnp.concatenate([top, x[mask.shape[0]:]], axis=0)

        ks = k_ref[pl.ds(start, blk), hp * LANES:(hp + 1) * LANES]
        vs = v_ref[pl.ds(start, blk), hp * LANES:(hp + 1) * LANES]
        z = lax.dot_general(qs[r0:], ks, (((1,), (1,)), ((), ())), preferred_element_type=F32)
        sp = jnp.maximum(z, jnp.log(1.0 + jnp.exp2(jnp.minimum(z, SP_LINEAR_FROM))) * LOG2_E)
        d = z - sp
        hi = masked(sp).astype(BF16)
        suffix = jnp.dot(hi, m, preferred_element_type=F32)
        ls_in = ls[r0:]
        w = masked(jnp.exp2(d - suffix + jnp.concatenate([ls_in, ls_in], axis=1)))
        acc_out = acc[r0:] + jnp.dot(w.astype(BF16), vs, preferred_element_type=F32)
        ls_out = ls_in - (suffix[:, :1] + hi[:, :1].astype(F32))
        if r0 == 0:
            return acc_out, ls_out
        return (jnp.concatenate([acc[:r0], acc_out], axis=0),
                jnp.concatenate([ls[:r0], ls_out], axis=0))

    zeros = jnp.zeros((2 * tq, LANES), F32)
    n_older = i * n_blk
    prev_start = pl.multiple_of(jnp.maximum(i * tq - blk, 0), blk)
    has_prev = (i > 0).astype(F32)

    qs_all, state = [], []
    for hp in range(n_pairs):
        qs = stacked_queries(hp)
        acc, ls = zeros, zeros
        for j in reversed(range(n_blk)):
            acc, ls = unit(hp, qs, pl.multiple_of(i * tq + j * blk, blk), acc, ls, 2 * j * blk,
                           mask=vis_ref[...])
        acc, ls = unit(hp, qs, prev_start, acc, ls, 0, gain=has_prev)
        qs_all.append(qs)
        state.append((acc, ls))

    for hp in range(n_pairs):
        qs = qs_all[hp]

        def more(c):
            return jnp.logical_and(c[0] < n_older, jnp.max(c[2]) > LS_FLOOR)

        def older_unit(c, hp=hp, qs=qs):
            n, acc, ls = c
            acc, ls = unit(hp, qs, pl.multiple_of((n_older - 1 - n) * blk, blk), acc, ls, 0)
            return n + 1, acc, ls

        _, acc, ls = lax.while_loop(more, older_unit, (jnp.minimum(i, 1),) + state[hp])
        per_head = [jnp.concatenate([acc[(2 * b + h) * blk:(2 * b + h + 1) * blk]
                                     for b in range(n_blk)], axis=0) for h in range(2)]
        o_ref[:, hp * LANES:(hp + 1) * LANES] = jnp.where(
            lane < SB_HEAD_DIM, per_head[0], per_head[1]).astype(o_ref.dtype)


def _causal_visibility():
    r = lax.broadcasted_iota(jnp.int32, (2 * SUFFIX_UNIT, SUFFIX_UNIT), 0)
    col = lax.broadcasted_iota(jnp.int32, (2 * SUFFIX_UNIT, SUFFIX_UNIT), 1)
    return (col < r % SUFFIX_UNIT).astype(F32)


def _attention(q, k, v, bsz, seq):
    t = q.shape[0]
    tq = ATTN_TQ
    nq = seq // tq
    cols = ATTN_PAIRS * LANES
    n_blk = SB_WIDTH // cols
    qspec = pl.BlockSpec((tq, cols), lambda b, h, i: (b * nq + i, h))
    kvspec = pl.BlockSpec((seq, cols), lambda b, h, i: (b, h))
    return pl.pallas_call(
        _attn_kernel,
        grid=(bsz, n_blk, nq),
        in_specs=[qspec, kvspec, kvspec, _const_spec((SUFFIX_UNIT, SUFFIX_UNIT)),
                  _const_spec((2 * SUFFIX_UNIT, SUFFIX_UNIT))],
        out_specs=qspec,
        out_shape=jax.ShapeDtypeStruct((t, SB_WIDTH), BF16),
        compiler_params=pltpu.CompilerParams(
            dimension_semantics=("parallel", "parallel", "arbitrary"),
            vmem_limit_bytes=VMEM_LIMIT),
        name="stickbreak_attention",
    )(q, k, v, _suffix_sum_matrix(), _causal_visibility())


def _merge_ln1_kernel(x_ref, oa_ref, ob_ref, ga_ref, gb_ref, wa_ref, wb_ref, wo_ref,
                      g_ref, b_ref, h_ref, *, alpha):
    sub = x_ref.shape[0] // 2
    for r0 in (0, sub):
        rows = slice(r0, r0 + sub)
        pa = jnp.dot(oa_ref[rows, :], wa_ref[...], preferred_element_type=F32)
        pb = jnp.dot(ob_ref[rows, :], wb_ref[...], preferred_element_type=F32)
        merged = ga_ref[rows, :].astype(F32) * pa + gb_ref[rows, :].astype(F32) * pb
        r = alpha * x_ref[rows, :] + jnp.dot(merged.astype(BF16), wo_ref[...],
                                             preferred_element_type=F32)
        h_ref[rows, :] = _layer_norm(r, g_ref[...], b_ref[...])


def _merge_ln1(x2, o_a, o_b, sg_a, sg_b, w_a, w_b, w_out, ln_g, ln_b, alpha):
    t, d = x2.shape
    tm = ROW_TILE
    row = lambda n: pl.BlockSpec((tm, n), lambda i: (i, 0))
    return pl.pallas_call(
        functools.partial(_merge_ln1_kernel, alpha=alpha),
        grid=(t // tm,),
        in_specs=[row(d), row(o_a.shape[1]), row(o_b.shape[1]), row(d), row(d),
                  _const_spec(w_a.shape), _const_spec(w_b.shape), _const_spec(w_out.shape),
                  _const_spec((1, d)), _const_spec((1, d))],
        out_specs=row(d),
        out_shape=jax.ShapeDtypeStruct((t, d), F32),
        compiler_params=pltpu.CompilerParams(dimension_semantics=("parallel",),
                                             vmem_limit_bytes=VMEM_LIMIT),
        name="merge_ln1",
    )(x2, o_a, o_b, sg_a, sg_b, w_a, w_b, w_out, ln_g.reshape(1, d), ln_b.reshape(1, d))


def _ffn_ple_ln2_kernel(h_ref, p_ref, wg_ref, wu_ref, wd_ref, wpg_ref, wpp_ref,
                        g_ref, b_ref, o_ref, *, alpha, ff_chunk):
    h = h_ref[...]
    hb = h.astype(BF16)
    d_ff = wg_ref.shape[1]
    ple_gate = jax.nn.sigmoid(jnp.dot(hb, wpg_ref[...], preferred_element_type=F32))
    ple = ple_gate * jnp.dot(p_ref[...].astype(BF16), wpp_ref[...], preferred_element_type=F32)
    ffn = alpha * h + ple
    for c0 in range(0, d_ff, ff_chunk):
        gate = jnp.dot(hb, wg_ref[:, c0:c0 + ff_chunk], preferred_element_type=F32)
        up = jnp.dot(hb, wu_ref[:, c0:c0 + ff_chunk], preferred_element_type=F32)
        act = (jax.nn.silu(gate) * up).astype(BF16)
        ffn = ffn + jnp.dot(act, wd_ref[c0:c0 + ff_chunk, :], preferred_element_type=F32)
    o_ref[...] = _layer_norm(ffn, g_ref[...], b_ref[...])


def _ffn_ple_ln2(h1, p2, w_gate, w_up, w_down, w_pg, w_pp, ln_g, ln_b, alpha):
    t, d = h1.shape
    tm = ROW_TILE
    d_ff = w_gate.shape[1]
    ff_chunk = 2 * LANES if d_ff % (2 * LANES) == 0 else d_ff
    row = lambda n: pl.BlockSpec((tm, n), lambda i: (i, 0))
    return pl.pallas_call(
        functools.partial(_ffn_ple_ln2_kernel, alpha=alpha, ff_chunk=ff_chunk),
        grid=(t // tm,),
        in_specs=[row(d), row(p2.shape[1]),
                  _const_spec(w_gate.shape), _const_spec(w_up.shape), _const_spec(w_down.shape),
                  _const_spec(w_pg.shape), _const_spec(w_pp.shape),
                  _const_spec((1, d)), _const_spec((1, d))],
        out_specs=row(d),
        out_shape=jax.ShapeDtypeStruct((t, d), F32),
        compiler_params=pltpu.CompilerParams(dimension_semantics=("parallel",),
                                             vmem_limit_bytes=VMEM_LIMIT),
        name="ffn_ple_ln2",
    )(h1, p2, w_gate, w_up, w_down, w_pg, w_pp, ln_g.reshape(1, d), ln_b.reshape(1, d))


def kernel(x, p, w_in, ssm_lambda_re, ssm_lambda_im, ssm_log_dt, ssm_b_re, ssm_b_im, ssm_c_re, ssm_c_im, ssm_d, ssm_glu_w, ssm_glu_b, w_branch_a, w_branch_b, w_out, ln1_g, ln1_b, ffn_w_gate, ffn_w_up, ffn_w_down, ple_w_gate, ple_w_proj, ln2_g, ln2_b):
    bsz, seq, d = x.shape
    depth = w_in.shape[0]
    alpha = (2 * depth) ** 0.25
    t = bsz * seq
    h = x.reshape(t, d)
    for i in range(depth):
        q, k, v, sg_a, sg_b, o_b = _in_proj_s5(
            h, seq, w_in[i].astype(BF16), ssm_lambda_re[i], ssm_lambda_im[i], ssm_log_dt[i],
            ssm_b_re[i], ssm_b_im[i], ssm_c_re[i], ssm_c_im[i], ssm_d[i], ssm_glu_w[i], ssm_glu_b[i])
        o_a = _attention(q, k, v, bsz, seq)
        h = _merge_ln1(h, o_a, o_b, sg_a, sg_b, w_branch_a[i].astype(BF16), w_branch_b[i].astype(BF16),
                       w_out[i].astype(BF16), ln1_g[i], ln1_b[i], alpha)
        h = _ffn_ple_ln2(h, p[i].reshape(t, -1), ffn_w_gate[i].astype(BF16), ffn_w_up[i].astype(BF16),
                         ffn_w_down[i].astype(BF16), ple_w_gate[i].astype(BF16),
                         ple_w_proj[i].astype(BF16), ln2_g[i], ln2_b[i], alpha)
    return h.reshape(bsz, seq, d)
```

```python
import functools

import jax
import jax.numpy as jnp
from jax import lax
from jax.experimental import pallas as pl
from jax.experimental.pallas import tpu as pltpu

F32 = jnp.float32
BF16 = jnp.bfloat16

LN_EPS = 1e-5
SB_HEADS = 8
SB_HEAD_DIM = 64
SB_WIDTH = SB_HEADS * SB_HEAD_DIM

LANES = 128
SUBLANES = 8
SUFFIX_UNIT = 2 * LANES
LOG2_E = 1.4426950408889634
SP_LINEAR_FROM = 126.0
LS_FLOOR = -200.0
VMEM_LIMIT = 56 * 1024 * 1024

ROW_TILE = 512
ATTN_TQ = 512
ATTN_PAIRS = 2


def _const_spec(shape):
    nd = len(shape)
    return pl.BlockSpec(shape, lambda *_: (0,) * nd, pipeline_mode=pl.Buffered(1))


def _layer_norm(r, g, b):
    mu = jnp.mean(r, axis=-1, keepdims=True)
    d = r - mu
    var = jnp.mean(d * d, axis=-1, keepdims=True)
    return d * lax.rsqrt(var + LN_EPS) * g + b


def _s5_discretize_kernel(lre_ref, lim_ref, ldt_ref, bre_ref, bim_ref, cre_ref, cim_ref,
                          a2re_ref, a2im_ref, bbre_ref, bbim_ref, abre_ref, abim_ref,
                          acre_ref, acim_ref, a2cre_ref, a2cim_ref):
    lam_re = lre_ref[...]
    lam_im = lim_ref[...]
    dt = jnp.exp(ldt_ref[...])
    mag = jnp.exp(lam_re * dt)
    a_re = mag * jnp.cos(lam_im * dt)
    a_im = mag * jnp.sin(lam_im * dt)
    den = lam_re * lam_re + lam_im * lam_im
    n_re = a_re - 1.0
    coef_re = (n_re * lam_re + a_im * lam_im) / den
    coef_im = (a_im * lam_re - n_re * lam_im) / den
    b_re = bre_ref[...]
    b_im = bim_ref[...]
    bb_re = coef_re * b_re - coef_im * b_im
    bb_im = coef_re * b_im + coef_im * b_re
    a2_re = a_re * a_re - a_im * a_im
    a2_im = 2.0 * a_re * a_im
    c_re = cre_ref[...]
    c_im = cim_ref[...]
    a2re_ref[...] = a2_re
    a2im_ref[...] = a2_im
    bbre_ref[...] = bb_re
    bbim_ref[...] = bb_im
    abre_ref[...] = a_re * bb_re - a_im * bb_im
    abim_ref[...] = a_re * bb_im + a_im * bb_re
    acre_ref[...] = a_re * c_re - a_im * c_im
    acim_ref[...] = a_re * c_im + a_im * c_re
    a2cre_ref[...] = a2_re * c_re - a2_im * c_im
    a2cim_ref[...] = a2_re * c_im + a2_im * c_re


def _s5_pair_terms_kernel(cre_ref, cim_ref, bbre_ref, bbim_ref, abre_ref, abim_ref, cb_ref, cab_ref):
    def contract(c_ref, x_ref):
        return lax.dot_general(c_ref[...], x_ref[...], (((2,), (1,)), ((0,), (0,))),
                               preferred_element_type=F32)

    cb_ref[...] = contract(cre_ref, bbre_ref) - contract(cim_ref, bbim_ref)
    cab_ref[...] = contract(cre_ref, abre_ref) - contract(cim_ref, abim_ref)


def _s5_constants(lam_re, lam_im, log_dt, b_re, b_im, c_re, c_im):
    g, p = lam_re.shape
    c = b_re.shape[-1]
    rep = lambda a: jnp.repeat(a, c, axis=1)
    flat = lambda a: a.reshape(g, p * c)
    unflat = lambda a: a.reshape(g, p, c)
    ldt = jnp.broadcast_to(log_dt[:, None], (g, p * c))
    ct_re, ct_im = c_re.transpose(0, 2, 1), c_im.transpose(0, 2, 1)
    shp = jax.ShapeDtypeStruct((g, p * c), F32)
    a2_re, a2_im, bb_re, bb_im, ab_re, ab_im, ac_re, ac_im, a2c_re, a2c_im = pl.pallas_call(
        _s5_discretize_kernel, out_shape=(shp,) * 10, name="s5_discretize",
    )(rep(lam_re), rep(lam_im), ldt, flat(b_re), flat(b_im), flat(ct_re), flat(ct_im))
    pair = jax.ShapeDtypeStruct((g, c, c), F32)
    cb, cab = pl.pallas_call(
        _s5_pair_terms_kernel, out_shape=(pair, pair), name="s5_pair_terms",
    )(c_re, c_im, unflat(bb_re), unflat(bb_im), unflat(ab_re), unflat(ab_im))
    return dict(a2_re=a2_re[:, ::c], a2_im=a2_im[:, ::c],
                bb_re=unflat(bb_re), bb_im=unflat(bb_im), ab_re=unflat(ab_re), ab_im=unflat(ab_im),
                ac_re=unflat(ac_re), ac_im=unflat(ac_im), a2c_re=unflat(a2c_re), a2c_im=unflat(a2c_im),
                cb=cb, cab=cab)


def _in_proj_s5_kernel(x_ref, w_ref, abre_ref, abim_ref, bbre_ref, bbim_ref,
                       acre_ref, acim_ref, a2cre_ref, a2cim_ref, cb_ref, cab_ref, are_ref, aim_ref,
                       d_ref, gw_ref, gbias_ref,
                       q_ref, k_ref, v_ref, ga_ref, gb_ref, ob_ref,
                       u_buf, y_buf, cre_st, cim_st, sre_buf, sim_buf, *, q_scale, tiles_per_seq):
    i = pl.program_id(0)
    tm, d_model = x_ref.shape
    n_ublk = u_buf.shape[1]
    width = n_ublk * LANES
    n_pair = tm // 2
    n_chunk = bbre_ref.shape[1] // LANES
    n_slab = n_chunk // SUBLANES
    ch_per_ublk = n_chunk // n_ublk

    @pl.when(i == 0)
    def _():
        u_buf[...] = jnp.zeros_like(u_buf)
        cre_st[...] = jnp.zeros_like(cre_st)
        cim_st[...] = jnp.zeros_like(cim_st)

    slot = i % 2

    def u_pairs(blk):
        even = u_buf[1 - slot, blk, pl.ds(0, n_pair, stride=2), :]
        odd = u_buf[1 - slot, blk, pl.ds(1, n_pair, stride=2), :]
        return jnp.concatenate([even, odd], axis=1).astype(BF16)

    def slab_rows(c):
        return pl.ds(c % SUBLANES, n_pair, stride=SUBLANES)

    def drive():
        for blk in range(n_ublk):
            up = u_pairs(blk)
            rows = slice(blk * LANES, (blk + 1) * LANES)
            cols = slice(blk * ch_per_ublk * LANES, (blk + 1) * ch_per_ublk * LANES)
            for ab_ref, bb_ref, buf in ((abre_ref, bbre_ref, sre_buf), (abim_ref, bbim_ref, sim_buf)):
                w2 = jnp.concatenate([ab_ref[rows, cols], bb_ref[rows, cols]], axis=0)
                val = jnp.dot(up, w2, preferred_element_type=F32)
                for e in range(ch_per_ublk):
                    c = blk * ch_per_ublk + e
                    buf[c // SUBLANES, slab_rows(c), :] = val[:, e * LANES:(e + 1) * LANES]

    a_re = [are_ref[k] for k in range(n_slab)]
    a_im = [aim_ref[k] for k in range(n_slab)]

    def scan(carry):
        for t in range(n_pair):
            tok = pl.ds(t * SUBLANES, SUBLANES)
            new = []
            for idx in range(n_slab):
                s_re, s_im = carry[idx]
                n_re = a_re[idx] * s_re - a_im[idx] * s_im + sre_buf[idx, tok, :]
                n_im = a_re[idx] * s_im + a_im[idx] * s_re + sim_buf[idx, tok, :]
                sre_buf[idx, tok, :] = s_re
                sim_buf[idx, tok, :] = s_im
                new.append((n_re, n_im))
            carry = new
        return carry

    def readout():
        for blk in range(n_ublk):
            cols = slice(blk * LANES, (blk + 1) * LANES)
            cb = cb_ref[cols, cols]
            intra = jnp.concatenate([jnp.concatenate([cb, cab_ref[cols, cols]], axis=1),
                                     jnp.concatenate([jnp.zeros_like(cb), cb], axis=1)], axis=0)
            acc = jnp.dot(u_pairs(blk), intra, preferred_element_type=F32)
            for half in range(ch_per_ublk // 2):
                c0 = blk * ch_per_ublk + 2 * half
                rows = slice(c0 * LANES, (c0 + 2) * LANES)

                def chunk_pair(buf):
                    parts = [buf[(c0 + e) // SUBLANES, slab_rows(c0 + e), :] for e in range(2)]
                    return jnp.concatenate(parts, axis=1).astype(BF16)

                r_re = jnp.concatenate([acre_ref[rows, cols], a2cre_ref[rows, cols]], axis=1)
                r_im = jnp.concatenate([acim_ref[rows, cols], a2cim_ref[rows, cols]], axis=1)
                acc = acc + jnp.dot(chunk_pair(sre_buf), r_re, preferred_element_type=F32)
                acc = acc - jnp.dot(chunk_pair(sim_buf), r_im, preferred_element_type=F32)
            y_buf[blk, pl.ds(0, n_pair, stride=2), :] = acc[:, :LANES]
            y_buf[blk, pl.ds(1, n_pair, stride=2), :] = acc[:, LANES:]
        y = jnp.concatenate([y_buf[blk] + d_ref[:, blk * LANES:(blk + 1) * LANES] * u_buf[1 - slot, blk]
                             for blk in range(n_ublk)], axis=1)
        y = jax.nn.gelu(y)
        gate = jnp.dot(y.astype(BF16), gw_ref[...], preferred_element_type=F32) + gbias_ref[...]
        ob_ref[...] = (y * jax.nn.sigmoid(gate)).astype(ob_ref.dtype)

    opens_sequence = (i + tiles_per_seq - 1) % tiles_per_seq == 0
    carry = [(jnp.where(opens_sequence, 0.0, cre_st[idx]), jnp.where(opens_sequence, 0.0, cim_st[idx]))
             for idx in range(n_slab)]

    xb = x_ref[...].astype(BF16)
    w = SB_WIDTH
    g0 = 3 * w + width
    half = d_model // 2

    def project(c0, n):
        return jnp.dot(xb, w_ref[:, c0:c0 + n], preferred_element_type=F32)

    def gate(out_ref, c0, lo):
        out_ref[:, lo:lo + half] = jax.nn.sigmoid(project(c0 + lo, half)).astype(BF16)

    q_ref[...] = (project(0, w) * q_scale).astype(BF16)
    drive()
    carry = scan(carry)
    k_ref[...] = project(w, w).astype(BF16)
    v_ref[...] = project(2 * w, w).astype(BF16)
    gate(ga_ref, g0, 0)
    readout()
    gate(ga_ref, g0, half)
    gate(gb_ref, g0 + d_model, 0)
    gate(gb_ref, g0 + d_model, half)
    for idx in range(n_slab):
        cre_st[idx] = carry[idx][0]
        cim_st[idx] = carry[idx][1]
    u_new = project(3 * w, width)
    for blk in range(n_ublk):
        u_buf[slot, blk] = u_new[:, blk * LANES:(blk + 1) * LANES]


def _block_diag(blocks, dtype):
    g, r, c = blocks.shape
    wide = blocks.transpose(1, 0, 2).reshape(r, g * c)
    row_g = lax.broadcasted_iota(jnp.int32, (g * r, g * c), 0) // r
    col_g = lax.broadcasted_iota(jnp.int32, (g * r, g * c), 1) // c
    return jnp.where(row_g == col_g, jnp.tile(wide, (g, 1)), 0.0).astype(dtype)


def _in_proj_s5(x2, seq, w_in_bf, lam_re, lam_im, log_dt, b_re, b_im, c_re, c_im, d_skip, glu_w, glu_b):
    t, d = x2.shape
    width = d_skip.shape[0]
    n_ublk = width // LANES
    k = _s5_constants(lam_re, lam_im, log_dt, b_re, b_im, c_re, c_im)
    n_state = k["a2_re"].size
    n_slab = n_state // (SUBLANES * LANES)
    to_states = lambda m: _block_diag(m.transpose(0, 2, 1), BF16)
    to_y = lambda m: _block_diag(m, BF16)
    a_re = k["a2_re"].reshape(n_slab, SUBLANES, LANES)
    a_im = k["a2_im"].reshape(n_slab, SUBLANES, LANES)
    tm = ROW_TILE
    n_tiles = t // tm
    cur = lambda n: pl.BlockSpec((tm, n), lambda i: (jnp.minimum(i, n_tiles - 1), 0))
    prev = lambda n: pl.BlockSpec((tm, n), lambda i: (jnp.maximum(i - 1, 0), 0))
    sds = lambda n, dt: jax.ShapeDtypeStruct((t, n), dt)
    consts = (w_in_bf, to_states(k["ab_re"]), to_states(k["ab_im"]), to_states(k["bb_re"]), to_states(k["bb_im"]),
              to_y(k["ac_re"]), to_y(k["ac_im"]), to_y(k["a2c_re"]), to_y(k["a2c_im"]),
              to_states(k["cb"]), to_states(k["cab"]), a_re, a_im,
              d_skip.reshape(1, width), glu_w.astype(BF16), glu_b.reshape(1, width))
    slab_buf = pltpu.VMEM((n_slab, (tm // 2) * SUBLANES, LANES), F32)
    carry = pltpu.VMEM((n_slab, SUBLANES, LANES), F32)
    return pl.pallas_call(
        functools.partial(_in_proj_s5_kernel, q_scale=LOG2_E * SB_HEAD_DIM ** -0.5,
                          tiles_per_seq=seq // tm),
        grid=(n_tiles + 1,),
        in_specs=[cur(d)] + [_const_spec(c.shape) for c in consts],
        out_specs=[cur(SB_WIDTH), cur(SB_WIDTH), cur(SB_WIDTH), cur(d), cur(d), prev(width)],
        out_shape=[sds(SB_WIDTH, BF16), sds(SB_WIDTH, BF16), sds(SB_WIDTH, BF16),
                   sds(d, BF16), sds(d, BF16), sds(width, BF16)],
        scratch_shapes=[pltpu.VMEM((2, n_ublk, tm, LANES), F32), pltpu.VMEM((n_ublk, tm, LANES), F32),
                        carry, carry, slab_buf, slab_buf],
        compiler_params=pltpu.CompilerParams(dimension_semantics=("arbitrary",),
                                             vmem_limit_bytes=VMEM_LIMIT),
        name="in_proj_s5",
    )(x2, *consts)


def _suffix_sum_matrix():
    j = lax.broadcasted_iota(jnp.int32, (SUFFIX_UNIT, SUFFIX_UNIT), 0)
    s = lax.broadcasted_iota(jnp.int32, (SUFFIX_UNIT, SUFFIX_UNIT), 1)
    return jnp.where(j > s, 1.0, 0.0).astype(BF16)


def _attn_kernel(q_ref, k_ref, v_ref, m_ref, vis_ref, o_ref):
    tq = q_ref.shape[0]
    blk = SUFFIX_UNIT
    n_blk = tq // blk
    n_pairs = q_ref.shape[1] // LANES
    i = pl.program_id(2)
    lane = lax.broadcasted_iota(jnp.int32, (1, LANES), 1)
    m = m_ref[...]

    def stacked_queries(hp):
        q2 = q_ref[:, hp * LANES:(hp + 1) * LANES]
        zero = jnp.zeros_like(q2)
        heads = (jnp.where(lane < SB_HEAD_DIM, q2, zero), jnp.where(lane >= SB_HEAD_DIM, q2, zero))
        return jnp.concatenate([h[b * blk:(b + 1) * blk] for b in range(n_blk) for h in heads], axis=0)

    def unit(hp, qs, start, acc, ls, r0, mask=None, gain=None):
        def masked(x):
            if mask is None:
                return x
            top = x[:mask.shape[0]] * mask
            return top if mask.shape[0] == x.shape[0] else jnp.concatenate([top, x[mask.shape[0]:]], axis=0)

        ks = k_ref[pl.ds(start, blk), hp * LANES:(hp + 1) * LANES]
        vs = v_ref[pl.ds(start, blk), hp * LANES:(hp + 1) * LANES]
        z = lax.dot_general(qs[r0:], ks, (((1,), (1,)), ((), ())), preferred_element_type=F32)
        sp = jnp.maximum(z, jnp.log(1.0 + jnp.exp2(jnp.minimum(z, SP_LINEAR_FROM))) * LOG2_E)
        d = z - sp
        hi = masked(sp).astype(BF16)
        if gain is not None:
            hi = hi * gain
        suffix = jnp.dot(hi, m, preferred_element_type=F32)
        ls_in = ls[r0:]
        w = masked(jnp.exp2(d - suffix + jnp.concatenate([ls_in, ls_in], axis=1))).astype(BF16)
        if gain is not None:
            w = w * gain
        acc_out = acc[r0:] + jnp.dot(w, vs, preferred_element_type=F32)
        ls_out = ls_in - (suffix[:, :1] + hi[:, :1].astype(F32))
        if r0 == 0:
            return acc_out, ls_out
        return (jnp.concatenate([acc[:r0], acc_out], axis=0),
                jnp.concatenate([ls[:r0], ls_out], axis=0))

    zeros = jnp.zeros((2 * tq, LANES), F32)
    n_older = i * n_blk
    prev_start = pl.multiple_of(jnp.maximum(i * tq - blk, 0), blk)
    has_prev = (i > 0).astype(BF16)

    qs_all, state = [], []
    for hp in range(n_pairs):
        qs = stacked_queries(hp)
        acc, ls = zeros, zeros
        for j in reversed(range(n_blk)):
            acc, ls = unit(hp, qs, pl.multiple_of(i * tq + j * blk, blk), acc, ls, 2 * j * blk,
                           mask=vis_ref[...])
        acc, ls = unit(hp, qs, prev_start, acc, ls, 0, gain=has_prev)
        qs_all.append(qs)
        state.append((acc, ls))

    for hp in range(n_pairs):
        qs = qs_all[hp]

        def more(c):
            return jnp.logical_and(c[0] < n_older, jnp.max(c[2]) > LS_FLOOR)

        def older_unit(c, hp=hp, qs=qs):
            n, acc, ls = c
            acc, ls = unit(hp, qs, pl.multiple_of((n_older - 1 - n) * blk, blk), acc, ls, 0)
            return n + 1, acc, ls

        _, acc, ls = lax.while_loop(more, older_unit, (jnp.minimum(i, 1),) + state[hp])
        per_head = [jnp.concatenate([acc[(2 * b + h) * blk:(2 * b + h + 1) * blk]
                                     for b in range(n_blk)], axis=0) for h in range(2)]
        o_ref[:, hp * LANES:(hp + 1) * LANES] = jnp.where(
            lane < SB_HEAD_DIM, per_head[0], per_head[1]).astype(o_ref.dtype)


def _causal_visibility():
    r = lax.broadcasted_iota(jnp.int32, (2 * SUFFIX_UNIT, SUFFIX_UNIT), 0)
    col = lax.broadcasted_iota(jnp.int32, (2 * SUFFIX_UNIT, SUFFIX_UNIT), 1)
    return (col < r % SUFFIX_UNIT).astype(F32)


def _attention(q, k, v, bsz, seq):
    t = q.shape[0]
    tq = ATTN_TQ
    nq = seq // tq
    cols = ATTN_PAIRS * LANES
    n_blk = SB_WIDTH // cols
    qspec = pl.BlockSpec((tq, cols), lambda b, h, i: (b * nq + i, h))
    kvspec = pl.BlockSpec((seq, cols), lambda b, h, i: (b, h))
    return pl.pallas_call(
        _attn_kernel,
        grid=(bsz, n_blk, nq),
        in_specs=[qspec, kvspec, kvspec, _const_spec((SUFFIX_UNIT, SUFFIX_UNIT)),
                  _const_spec((2 * SUFFIX_UNIT, SUFFIX_UNIT))],
        out_specs=qspec,
        out_shape=jax.ShapeDtypeStruct((t, SB_WIDTH), BF16),
        compiler_params=pltpu.CompilerParams(
            dimension_semantics=("parallel", "parallel", "arbitrary"),
            vmem_limit_bytes=VMEM_LIMIT),
        name="stickbreak_attention",
    )(q, k, v, _suffix_sum_matrix(), _causal_visibility())


def _merge_ln1_kernel(x_ref, oa_ref, ob_ref, ga_ref, gb_ref, wa_ref, wb_ref, wo_ref,
                      g_ref, b_ref, h_ref, *, alpha):
    sub = x_ref.shape[0] // 2
    for r0 in (0, sub):
        rows = slice(r0, r0 + sub)
        pa = jnp.dot(oa_ref[rows, :], wa_ref[...], preferred_element_type=F32)
        pb = jnp.dot(ob_ref[rows, :], wb_ref[...], preferred_element_type=F32)
        merged = ga_ref[rows, :].astype(F32) * pa + gb_ref[rows, :].astype(F32) * pb
        r = alpha * x_ref[rows, :] + jnp.dot(merged.astype(BF16), wo_ref[...],
                                             preferred_element_type=F32)
        h_ref[rows, :] = _layer_norm(r, g_ref[...], b_ref[...])


def _merge_ln1(x2, o_a, o_b, sg_a, sg_b, w_a, w_b, w_out, ln_g, ln_b, alpha):
    t, d = x2.shape
    tm = ROW_TILE
    row = lambda n: pl.BlockSpec((tm, n), lambda i: (i, 0))
    return pl.pallas_call(
        functools.partial(_merge_ln1_kernel, alpha=alpha),
        grid=(t // tm,),
        in_specs=[row(d), row(o_a.shape[1]), row(o_b.shape[1]), row(d), row(d),
                  _const_spec(w_a.shape), _const_spec(w_b.shape), _const_spec(w_out.shape),
                  _const_spec((1, d)), _const_spec((1, d))],
        out_specs=row(d),
        out_shape=jax.ShapeDtypeStruct((t, d), F32),
        compiler_params=pltpu.CompilerParams(dimension_semantics=("parallel",),
                                             vmem_limit_bytes=VMEM_LIMIT),
        name="merge_ln1",
    )(x2, o_a, o_b, sg_a, sg_b, w_a, w_b, w_out, ln_g.reshape(1, d), ln_b.reshape(1, d))


def _ffn_ple_ln2_kernel(h_ref, p_ref, wg_ref, wu_ref, wd_ref, wpg_ref, wpp_ref,
                        g_ref, b_ref, o_ref, *, alpha, ff_chunk):
    h = h_ref[...]
    hb = h.astype(BF16)
    d_ff = wg_ref.shape[1]
    ple_gate = jax.nn.sigmoid(jnp.dot(hb, wpg_ref[...], preferred_element_type=F32))
    ple = ple_gate * jnp.dot(p_ref[...].astype(BF16), wpp_ref[...], preferred_element_type=F32)
    ffn = alpha * h + ple
    for c0 in range(0, d_ff, ff_chunk):
        gate = jnp.dot(hb, wg_ref[:, c0:c0 + ff_chunk], preferred_element_type=F32)
        up = jnp.dot(hb, wu_ref[:, c0:c0 + ff_chunk], preferred_element_type=F32)
        act = (jax.nn.silu(gate) * up).astype(BF16)
        ffn = ffn + jnp.dot(act, wd_ref[c0:c0 + ff_chunk, :], preferred_element_type=F32)
    o_ref[...] = _layer_norm(ffn, g_ref[...], b_ref[...])


def _ffn_ple_ln2(h1, p2, w_gate, w_up, w_down, w_pg, w_pp, ln_g, ln_b, alpha):
    t, d = h1.shape
    tm = ROW_TILE
    d_ff = w_gate.shape[1]
    ff_chunk = 2 * LANES if d_ff % (2 * LANES) == 0 else d_ff
    row = lambda n: pl.BlockSpec((tm, n), lambda i: (i, 0))
    return pl.pallas_call(
        functools.partial(_ffn_ple_ln2_kernel, alpha=alpha, ff_chunk=ff_chunk),
        grid=(t // tm,),
        in_specs=[row(d), row(p2.shape[1]),
                  _const_spec(w_gate.shape), _const_spec(w_up.shape), _const_spec(w_down.shape),
                  _const_spec(w_pg.shape), _const_spec(w_pp.shape),
                  _const_spec((1, d)), _const_spec((1, d))],
        out_specs=row(d),
        out_shape=jax.ShapeDtypeStruct((t, d), F32),
        compiler_params=pltpu.CompilerParams(dimension_semantics=("parallel",),
                                             vmem_limit_bytes=VMEM_LIMIT),
        name="ffn_ple_ln2",
    )(h1, p2, w_gate, w_up, w_down, w_pg, w_pp, ln_g.reshape(1, d), ln_b.reshape(1, d))


def kernel(x, p, w_in, ssm_lambda_re, ssm_lambda_im, ssm_log_dt, ssm_b_re, ssm_b_im, ssm_c_re, ssm_c_im, ssm_d, ssm_glu_w, ssm_glu_b, w_branch_a, w_branch_b, w_out, ln1_g, ln1_b, ffn_w_gate, ffn_w_up, ffn_w_down, ple_w_gate, ple_w_proj, ln2_g, ln2_b):
    bsz, seq, d = x.shape
    depth = w_in.shape[0]
    alpha = (2 * depth) ** 0.25
    t = bsz * seq
    h = x.reshape(t, d)
    for i in range(depth):
        q, k, v, sg_a, sg_b, o_b = _in_proj_s5(
            h, seq, w_in[i].astype(BF16), ssm_lambda_re[i], ssm_lambda_im[i], ssm_log_dt[i],
            ssm_b_re[i], ssm_b_im[i], ssm_c_re[i], ssm_c_im[i], ssm_d[i], ssm_glu_w[i], ssm_glu_b[i])
        o_a = _attention(q, k, v, bsz, seq)
        h = _merge_ln1(h, o_a, o_b, sg_a, sg_b, w_branch_a[i].astype(BF16), w_branch_b[i].astype(BF16),
                       w_out[i].astype(BF16), ln1_g[i], ln1_b[i], alpha)
        h = _ffn_ple_ln2(h, p[i].reshape(t, -1), ffn_w_gate[i].astype(BF16), ffn_w_up[i].astype(BF16),
                         ffn_w_down[i].astype(BF16), ple_w_gate[i].astype(BF16),
                         ple_w_proj[i].astype(BF16), ln2_g[i], ln2_b[i], alpha)
    return h.reshape(bsz, seq, d)
```

```python
import functools

import jax
import jax.numpy as jnp
from jax import lax
from jax.experimental import pallas as pl
from jax.experimental.pallas import tpu as pltpu

F32 = jnp.float32
BF16 = jnp.bfloat16

LN_EPS = 1e-5
SB_HEADS = 8
SB_HEAD_DIM = 64
SB_WIDTH = SB_HEADS * SB_HEAD_DIM

LANES = 128
SUBLANES = 8
SUFFIX_UNIT = 2 * LANES
LOG2_E = 1.4426950408889634
SP_LINEAR_FROM = 126.0
LS_FLOOR = -200.0
VMEM_LIMIT = 56 * 1024 * 1024

ROW_TILE = 512
ATTN_TQ = 512
ATTN_PAIRS = 2


def _const_spec(shape):
    nd = len(shape)
    return pl.BlockSpec(shape, lambda *_: (0,) * nd, pipeline_mode=pl.Buffered(1))


def _layer_norm(r, g, b):
    mu = jnp.mean(r, axis=-1, keepdims=True)
    d = r - mu
    var = jnp.mean(d * d, axis=-1, keepdims=True)
    return d * lax.rsqrt(var + LN_EPS) * g + b


def _s5_discretize_kernel(lre_ref, lim_ref, ldt_ref, bre_ref, bim_ref, cre_ref, cim_ref,
                          a2re_ref, a2im_ref, bbre_ref, bbim_ref, abre_ref, abim_ref,
                          acre_ref, acim_ref, a2cre_ref, a2cim_ref):
    lam_re = lre_ref[...]
    lam_im = lim_ref[...]
    dt = jnp.exp(ldt_ref[...])
    mag = jnp.exp(lam_re * dt)
    a_re = mag * jnp.cos(lam_im * dt)
    a_im = mag * jnp.sin(lam_im * dt)
    den = lam_re * lam_re + lam_im * lam_im
    n_re = a_re - 1.0
    coef_re = (n_re * lam_re + a_im * lam_im) / den
    coef_im = (a_im * lam_re - n_re * lam_im) / den
    b_re = bre_ref[...]
    b_im = bim_ref[...]
    bb_re = coef_re * b_re - coef_im * b_im
    bb_im = coef_re * b_im + coef_im * b_re
    a2_re = a_re * a_re - a_im * a_im
    a2_im = 2.0 * a_re * a_im
    c_re = cre_ref[...]
    c_im = cim_ref[...]
    a2re_ref[...] = a2_re
    a2im_ref[...] = a2_im
    bbre_ref[...] = bb_re
    bbim_ref[...] = bb_im
    abre_ref[...] = a_re * bb_re - a_im * bb_im
    abim_ref[...] = a_re * bb_im + a_im * bb_re
    acre_ref[...] = a_re * c_re - a_im * c_im
    acim_ref[...] = a_re * c_im + a_im * c_re
    a2cre_ref[...] = a2_re * c_re - a2_im * c_im
    a2cim_ref[...] = a2_re * c_im + a2_im * c_re


def _s5_pair_terms_kernel(cre_ref, cim_ref, bbre_ref, bbim_ref, abre_ref, abim_ref, cb_ref, cab_ref):
    def contract(c_ref, x_ref):
        return lax.dot_general(c_ref[...], x_ref[...], (((2,), (1,)), ((0,), (0,))),
                               preferred_element_type=F32)

    cb_ref[...] = contract(cre_ref, bbre_ref) - contract(cim_ref, bbim_ref)
    cab_ref[...] = contract(cre_ref, abre_ref) - contract(cim_ref, abim_ref)


def _s5_constants(lam_re, lam_im, log_dt, b_re, b_im, c_re, c_im):
    g, p = lam_re.shape
    c = b_re.shape[-1]
    rep = lambda a: jnp.repeat(a, c, axis=1)
    flat = lambda a: a.reshape(g, p * c)
    unflat = lambda a: a.reshape(g, p, c)
    ldt = jnp.broadcast_to(log_dt[:, None], (g, p * c))
    ct_re, ct_im = c_re.transpose(0, 2, 1), c_im.transpose(0, 2, 1)
    shp = jax.ShapeDtypeStruct((g, p * c), F32)
    a2_re, a2_im, bb_re, bb_im, ab_re, ab_im, ac_re, ac_im, a2c_re, a2c_im = pl.pallas_call(
        _s5_discretize_kernel, out_shape=(shp,) * 10, name="s5_discretize",
    )(rep(lam_re), rep(lam_im), ldt, flat(b_re), flat(b_im), flat(ct_re), flat(ct_im))
    pair = jax.ShapeDtypeStruct((g, c, c), F32)
    cb, cab = pl.pallas_call(
        _s5_pair_terms_kernel, out_shape=(pair, pair), name="s5_pair_terms",
    )(c_re, c_im, unflat(bb_re), unflat(bb_im), unflat(ab_re), unflat(ab_im))
    return dict(a2_re=a2_re[:, ::c], a2_im=a2_im[:, ::c],
                bb_re=unflat(bb_re), bb_im=unflat(bb_im), ab_re=unflat(ab_re), ab_im=unflat(ab_im),
                ac_re=unflat(ac_re), ac_im=unflat(ac_im), a2c_re=unflat(a2c_re), a2c_im=unflat(a2c_im),
                cb=cb, cab=cab)


def _in_proj_s5_kernel(x_ref, w_ref, abre_ref, abim_ref, bbre_ref, bbim_ref,
                       acre_ref, acim_ref, a2cre_ref, a2cim_ref, cb_ref, cab_ref, are_ref, aim_ref,
                       d_ref, gw_ref, gbias_ref,
                       q_ref, k_ref, v_ref, ga_ref, gb_ref, ob_ref,
                       u_buf, y_buf, cre_st, cim_st, sre_buf, sim_buf, *, q_scale, tiles_per_seq):
    i = pl.program_id(0)
    tm, d_model = x_ref.shape
    n_ublk = u_buf.shape[1]
    width = n_ublk * LANES
    n_pair = tm // 2
    n_chunk = bbre_ref.shape[1] // LANES
    n_slab = n_chunk // SUBLANES
    ch_per_ublk = n_chunk // n_ublk

    @pl.when(i == 0)
    def _():
        u_buf[...] = jnp.zeros_like(u_buf)
        cre_st[...] = jnp.zeros_like(cre_st)
        cim_st[...] = jnp.zeros_like(cim_st)

    slot = i % 2

    def u_pairs(blk):
        even = u_buf[1 - slot, blk, pl.ds(0, n_pair, stride=2), :]
        odd = u_buf[1 - slot, blk, pl.ds(1, n_pair, stride=2), :]
        return jnp.concatenate([even, odd], axis=1).astype(BF16)

    def slab_rows(c):
        return pl.ds(c % SUBLANES, n_pair, stride=SUBLANES)

    def drive():
        for blk in range(n_ublk):
            up = u_pairs(blk)
            rows = slice(blk * LANES, (blk + 1) * LANES)
            cols = slice(blk * ch_per_ublk * LANES, (blk + 1) * ch_per_ublk * LANES)
            for ab_ref, bb_ref, buf in ((abre_ref, bbre_ref, sre_buf), (abim_ref, bbim_ref, sim_buf)):
                w2 = jnp.concatenate([ab_ref[rows, cols], bb_ref[rows, cols]], axis=0)
                val = jnp.dot(up, w2, preferred_element_type=F32)
                for e in range(ch_per_ublk):
                    c = blk * ch_per_ublk + e
                    buf[c // SUBLANES, slab_rows(c), :] = val[:, e * LANES:(e + 1) * LANES]

    a_re = [are_ref[k] for k in range(n_slab)]
    a_im = [aim_ref[k] for k in range(n_slab)]

    def scan(carry):
        for t in range(n_pair):
            tok = pl.ds(t * SUBLANES, SUBLANES)
            new = []
            for idx in range(n_slab):
                s_re, s_im = carry[idx]
                n_re = a_re[idx] * s_re - a_im[idx] * s_im + sre_buf[idx, tok, :]
                n_im = a_re[idx] * s_im + a_im[idx] * s_re + sim_buf[idx, tok, :]
                sre_buf[idx, tok, :] = s_re
                sim_buf[idx, tok, :] = s_im
                new.append((n_re, n_im))
            carry = new
        return carry

    def readout():
        for blk in range(n_ublk):
            cols = slice(blk * LANES, (blk + 1) * LANES)
            cb = cb_ref[cols, cols]
            intra = jnp.concatenate([jnp.concatenate([cb, cab_ref[cols, cols]], axis=1),
                                     jnp.concatenate([jnp.zeros_like(cb), cb], axis=1)], axis=0)
            acc = jnp.dot(u_pairs(blk), intra, preferred_element_type=F32)
            for half in range(ch_per_ublk // 2):
                c0 = blk * ch_per_ublk + 2 * half
                rows = slice(c0 * LANES, (c0 + 2) * LANES)

                def chunk_pair(buf):
                    parts = [buf[(c0 + e) // SUBLANES, slab_rows(c0 + e), :] for e in range(2)]
                    return jnp.concatenate(parts, axis=1).astype(BF16)

                r_re = jnp.concatenate([acre_ref[rows, cols], a2cre_ref[rows, cols]], axis=1)
                r_im = jnp.concatenate([acim_ref[rows, cols], a2cim_ref[rows, cols]], axis=1)
                acc = acc + jnp.dot(chunk_pair(sre_buf), r_re, preferred_element_type=F32)
                acc = acc - jnp.dot(chunk_pair(sim_buf), r_im, preferred_element_type=F32)
            y_buf[blk, pl.ds(0, n_pair, stride=2), :] = acc[:, :LANES]
            y_buf[blk, pl.ds(1, n_pair, stride=2), :] = acc[:, LANES:]
        y = jnp.concatenate([y_buf[blk] + d_ref[:, blk * LANES:(blk + 1) * LANES] * u_buf[1 - slot, blk]
                             for blk in range(n_ublk)], axis=1)
        y = jax.nn.gelu(y)
        gate = jnp.dot(y.astype(BF16), gw_ref[...], preferred_element_type=F32) + gbias_ref[...]
        ob_ref[...] = (y * jax.nn.sigmoid(gate)).astype(ob_ref.dtype)

    opens_sequence = (i + tiles_per_seq - 1) % tiles_per_seq == 0
    carry = [(jnp.where(opens_sequence, 0.0, cre_st[idx]), jnp.where(opens_sequence, 0.0, cim_st[idx]))
             for idx in range(n_slab)]

    xb = x_ref[...].astype(BF16)
    w = SB_WIDTH
    g0 = 3 * w + width
    half = d_model // 2

    def project(c0, n):
        return jnp.dot(xb, w_ref[:, c0:c0 + n], preferred_element_type=F32)

    def gate(out_ref, c0, lo):
        out_ref[:, lo:lo + half] = jax.nn.sigmoid(project(c0 + lo, half)).astype(BF16)

    q_ref[...] = (project(0, w) * q_scale).astype(BF16)
    drive()
    carry = scan(carry)
    k_ref[...] = project(w, w).astype(BF16)
    v_ref[...] = project(2 * w, w).astype(BF16)
    readout()
    gate(ga_ref, g0, 0)
    gate(ga_ref, g0, half)
    gate(gb_ref, g0 + d_model, 0)
    gate(gb_ref, g0 + d_model, half)
    for idx in range(n_slab):
        cre_st[idx] = carry[idx][0]
        cim_st[idx] = carry[idx][1]
    u_new = project(3 * w, width)
    for blk in range(n_ublk):
        u_buf[slot, blk] = u_new[:, blk * LANES:(blk + 1) * LANES]


def _block_diag(blocks, dtype):
    g, r, c = blocks.shape
    wide = blocks.transpose(1, 0, 2).reshape(r, g * c)
    row_g = lax.broadcasted_iota(jnp.int32, (g * r, g * c), 0) // r
    col_g = lax.broadcasted_iota(jnp.int32, (g * r, g * c), 1) // c
    return jnp.where(row_g == col_g, jnp.tile(wide, (g, 1)), 0.0).astype(dtype)


def _in_proj_s5(x2, seq, w_in_bf, lam_re, lam_im, log_dt, b_re, b_im, c_re, c_im, d_skip, glu_w, glu_b):
    t, d = x2.shape
    width = d_skip.shape[0]
    n_ublk = width // LANES
    k = _s5_constants(lam_re, lam_im, log_dt, b_re, b_im, c_re, c_im)
    n_state = k["a2_re"].size
    n_slab = n_state // (SUBLANES * LANES)
    to_states = lambda m: _block_diag(m.transpose(0, 2, 1), BF16)
    to_y = lambda m: _block_diag(m, BF16)
    a_re = k["a2_re"].reshape(n_slab, SUBLANES, LANES)
    a_im = k["a2_im"].reshape(n_slab, SUBLANES, LANES)
    tm = ROW_TILE
    n_tiles = t // tm
    cur = lambda n: pl.BlockSpec((tm, n), lambda i: (jnp.minimum(i, n_tiles - 1), 0))
    prev = lambda n: pl.BlockSpec((tm, n), lambda i: (jnp.maximum(i - 1, 0), 0))
    sds = lambda n, dt: jax.ShapeDtypeStruct((t, n), dt)
    consts = (w_in_bf, to_states(k["ab_re"]), to_states(k["ab_im"]), to_states(k["bb_re"]), to_states(k["bb_im"]),
              to_y(k["ac_re"]), to_y(k["ac_im"]), to_y(k["a2c_re"]), to_y(k["a2c_im"]),
              to_states(k["cb"]), to_states(k["cab"]), a_re, a_im,
              d_skip.reshape(1, width), glu_w.astype(BF16), glu_b.reshape(1, width))
    slab_buf = pltpu.VMEM((n_slab, (tm // 2) * SUBLANES, LANES), F32)
    carry = pltpu.VMEM((n_slab, SUBLANES, LANES), F32)
    return pl.pallas_call(
        functools.partial(_in_proj_s5_kernel, q_scale=LOG2_E * SB_HEAD_DIM ** -0.5,
                          tiles_per_seq=seq // tm),
        grid=(n_tiles + 1,),
        in_specs=[cur(d)] + [_const_spec(c.shape) for c in consts],
        out_specs=[cur(SB_WIDTH), cur(SB_WIDTH), cur(SB_WIDTH), cur(d), cur(d), prev(width)],
        out_shape=[sds(SB_WIDTH, BF16), sds(SB_WIDTH, BF16), sds(SB_WIDTH, BF16),
                   sds(d, BF16), sds(d, BF16), sds(width, BF16)],
        scratch_shapes=[pltpu.VMEM((2, n_ublk, tm, LANES), F32), pltpu.VMEM((n_ublk, tm, LANES), F32),
                        carry, carry, slab_buf, slab_buf],
        compiler_params=pltpu.CompilerParams(dimension_semantics=("arbitrary",),
                                             vmem_limit_bytes=VMEM_LIMIT),
        name="in_proj_s5",
    )(x2, *consts)


def _suffix_sum_matrix():
    j = lax.broadcasted_iota(jnp.int32, (SUFFIX_UNIT, SUFFIX_UNIT), 0)
    s = lax.broadcasted_iota(jnp.int32, (SUFFIX_UNIT, SUFFIX_UNIT), 1)
    return jnp.where(j > s, 1.0, 0.0).astype(BF16)


def _attn_kernel(q_ref, k_ref, v_ref, m_ref, vis_ref, o_ref):
    tq = q_ref.shape[0]
    blk = SUFFIX_UNIT
    n_blk = tq // blk
    n_pairs = q_ref.shape[1] // LANES
    i = pl.program_id(2)
    lane = lax.broadcasted_iota(jnp.int32, (1, LANES), 1)
    m = m_ref[...]

    def stacked_queries(hp):
        q2 = q_ref[:, hp * LANES:(hp + 1) * LANES]
        zero = jnp.zeros_like(q2)
        heads = (jnp.where(lane < SB_HEAD_DIM, q2, zero), jnp.where(lane >= SB_HEAD_DIM, q2, zero))
        return jnp.concatenate([h[b * blk:(b + 1) * blk] for b in range(n_blk) for h in heads], axis=0)

    def unit(hp, qs, start, acc, ls, r0, mask=None, gain=None):
        def masked(x):
            if mask is None:
                return x
            top = x[:mask.shape[0]] * mask
            return top if mask.shape[0] == x.shape[0] else jnp.concatenate([top, x[mask.shape[0]:]], axis=0)

        ks = k_ref[pl.ds(start, blk), hp * LANES:(hp + 1) * LANES]
        vs = v_ref[pl.ds(start, blk), hp * LANES:(hp + 1) * LANES]
        z = lax.dot_general(qs[r0:], ks, (((1,), (1,)), ((), ())), preferred_element_type=F32)
        sp = jnp.maximum(z, jnp.log(1.0 + jnp.exp2(jnp.minimum(z, SP_LINEAR_FROM))) * LOG2_E)
        d = z - sp
        hi = masked(sp).astype(BF16)
        if gain is not None:
            hi = hi * gain
        suffix = jnp.dot(hi, m, preferred_element_type=F32)
        ls_in = ls[r0:]
        w = masked(jnp.exp2(d - suffix + jnp.concatenate([ls_in, ls_in], axis=1))).astype(BF16)
        if gain is not None:
            w = w * gain
        acc_out = acc[r0:] + jnp.dot(w, vs, preferred_element_type=F32)
        ls_out = ls_in - (suffix[:, :1] + hi[:, :1].astype(F32))
        if r0 == 0:
            return acc_out, ls_out
        return (jnp.concatenate([acc[:r0], acc_out], axis=0),
                jnp.concatenate([ls[:r0], ls_out], axis=0))

    zeros = jnp.zeros((2 * tq, LANES), F32)
    n_older = i * n_blk
    prev_start = pl.multiple_of(jnp.maximum(i * tq - blk, 0), blk)
    has_prev = (i > 0).astype(BF16)

    qs_all, state = [], []
    for hp in range(n_pairs):
        qs = stacked_queries(hp)
        acc, ls = zeros, zeros
        for j in reversed(range(n_blk)):
            acc, ls = unit(hp, qs, pl.multiple_of(i * tq + j * blk, blk), acc, ls, 2 * j * blk,
                           mask=vis_ref[...])
        acc, ls = unit(hp, qs, prev_start, acc, ls, 0, gain=has_prev)
        qs_all.append(qs)
        state.append((acc, ls))

    for hp in range(n_pairs):
        qs = qs_all[hp]

        def more(c):
            return jnp.logical_and(c[0] < n_older, jnp.max(c[2]) > LS_FLOOR)

        def older_unit(c, hp=hp, qs=qs):
            n, acc, ls = c
            acc, ls = unit(hp, qs, pl.multiple_of((n_older - 1 - n) * blk, blk), acc, ls, 0)
            return n + 1, acc, ls

        _, acc, ls = lax.while_loop(more, older_unit, (jnp.minimum(i, 1),) + state[hp])
        per_head = [jnp.concatenate([acc[(2 * b + h) * blk:(2 * b + h + 1) * blk]
                                     for b in range(n_blk)], axis=0) for h in range(2)]
        o_ref[:, hp * LANES:(hp + 1) * LANES] = jnp.where(
            lane < SB_HEAD_DIM, per_head[0], per_head[1]).astype(o_ref.dtype)


def _causal_visibility():
    r = lax.broadcasted_iota(jnp.int32, (2 * SUFFIX_UNIT, SUFFIX_UNIT), 0)
    col = lax.broadcasted_iota(jnp.int32, (2 * SUFFIX_UNIT, SUFFIX_UNIT), 1)
    return (col < r % SUFFIX_UNIT).astype(F32)


def _attention(q, k, v, bsz, seq):
    t = q.shape[0]
    tq = ATTN_TQ
    nq = seq // tq
    cols = ATTN_PAIRS * LANES
    n_blk = SB_WIDTH // cols
    qspec = pl.BlockSpec((tq, cols), lambda b, h, i: (b * nq + i, h))
    kvspec = pl.BlockSpec((seq, cols), lambda b, h, i: (b, h))
    return pl.pallas_call(
        _attn_kernel,
        grid=(bsz, n_blk, nq),
        in_specs=[qspec, kvspec, kvspec, _const_spec((SUFFIX_UNIT, SUFFIX_UNIT)),
                  _const_spec((2 * SUFFIX_UNIT, SUFFIX_UNIT))],
        out_specs=qspec,
        out_shape=jax.ShapeDtypeStruct((t, SB_WIDTH), BF16),
        compiler_params=pltpu.CompilerParams(
            dimension_semantics=("parallel", "parallel", "arbitrary"),
            vmem_limit_bytes=VMEM_LIMIT),
        name="stickbreak_attention",
    )(q, k, v, _suffix_sum_matrix(), _causal_visibility())


def _merge_ln1_kernel(x_ref, oa_ref, ob_ref, ga_ref, gb_ref, wa_ref, wb_ref, wo_ref,
                      g_ref, b_ref, h_ref, *, alpha):
    sub = x_ref.shape[0] // 2
    for r0 in (0, sub):
        rows = slice(r0, r0 + sub)
        pa = jnp.dot(oa_ref[rows, :], wa_ref[...], preferred_element_type=F32)
        pb = jnp.dot(ob_ref[rows, :], wb_ref[...], preferred_element_type=F32)
        merged = ga_ref[rows, :].astype(F32) * pa + gb_ref[rows, :].astype(F32) * pb
        r = alpha * x_ref[rows, :] + jnp.dot(merged.astype(BF16), wo_ref[...],
                                             preferred_element_type=F32)
        h_ref[rows, :] = _layer_norm(r, g_ref[...], b_ref[...])


def _merge_ln1(x2, o_a, o_b, sg_a, sg_b, w_a, w_b, w_out, ln_g, ln_b, alpha):
    t, d = x2.shape
    tm = 2 * ROW_TILE
    row = lambda n: pl.BlockSpec((tm, n), lambda i: (i, 0))
    return pl.pallas_call(
        functools.partial(_merge_ln1_kernel, alpha=alpha),
        grid=(t // tm,),
        in_specs=[row(d), row(o_a.shape[1]), row(o_b.shape[1]), row(d), row(d),
                  _const_spec(w_a.shape), _const_spec(w_b.shape), _const_spec(w_out.shape),
                  _const_spec((1, d)), _const_spec((1, d))],
        out_specs=row(d),
        out_shape=jax.ShapeDtypeStruct((t, d), F32),
        compiler_params=pltpu.CompilerParams(dimension_semantics=("parallel",),
                                             vmem_limit_bytes=VMEM_LIMIT),
        name="merge_ln1",
    )(x2, o_a, o_b, sg_a, sg_b, w_a, w_b, w_out, ln_g.reshape(1, d), ln_b.reshape(1, d))


def _ffn_ple_ln2_kernel(h_ref, p_ref, wg_ref, wu_ref, wd_ref, wpg_ref, wpp_ref,
                        g_ref, b_ref, o_ref, *, alpha, ff_chunk):
    h = h_ref[...]
    hb = h.astype(BF16)
    d_ff = wg_ref.shape[1]
    ple_gate = jax.nn.sigmoid(jnp.dot(hb, wpg_ref[...], preferred_element_type=F32))
    ple = ple_gate * jnp.dot(p_ref[...].astype(BF16), wpp_ref[...], preferred_element_type=F32)
    ffn = alpha * h + ple
    for c0 in range(0, d_ff, ff_chunk):
        gate = jnp.dot(hb, wg_ref[:, c0:c0 + ff_chunk], preferred_element_type=F32)
        up = jnp.dot(hb, wu_ref[:, c0:c0 + ff_chunk], preferred_element_type=F32)
        act = (jax.nn.silu(gate) * up).astype(BF16)
        ffn = ffn + jnp.dot(act, wd_ref[c0:c0 + ff_chunk, :], preferred_element_type=F32)
    o_ref[...] = _layer_norm(ffn, g_ref[...], b_ref[...])


def _ffn_ple_ln2(h1, p2, w_gate, w_up, w_down, w_pg, w_pp, ln_g, ln_b, alpha):
    t, d = h1.shape
    tm = ROW_TILE
    d_ff = w_gate.shape[1]
    ff_chunk = 2 * LANES if d_ff % (2 * LANES) == 0 else d_ff
    row = lambda n: pl.BlockSpec((tm, n), lambda i: (i, 0))
    return pl.pallas_call(
        functools.partial(_ffn_ple_ln2_kernel, alpha=alpha, ff_chunk=ff_chunk),
        grid=(t // tm,),
        in_specs=[row(d), row(p2.shape[1]),
                  _const_spec(w_gate.shape), _const_spec(w_up.shape), _const_spec(w_down.shape),
                  _const_spec(w_pg.shape), _const_spec(w_pp.shape),
                  _const_spec((1, d)), _const_spec((1, d))],
        out_specs=row(d),
        out_shape=jax.ShapeDtypeStruct((t, d), F32),
        compiler_params=pltpu.CompilerParams(dimension_semantics=("parallel",),
                                             vmem_limit_bytes=VMEM_LIMIT),
        name="ffn_ple_ln2",
    )(h1, p2, w_gate, w_up, w_down, w_pg, w_pp, ln_g.reshape(1, d), ln_b.reshape(1, d))


def kernel(x, p, w_in, ssm_lambda_re, ssm_lambda_im, ssm_log_dt, ssm_b_re, ssm_b_im, ssm_c_re, ssm_c_im, ssm_d, ssm_glu_w, ssm_glu_b, w_branch_a, w_branch_b, w_out, ln1_g, ln1_b, ffn_w_gate, ffn_w_up, ffn_w_down, ple_w_gate, ple_w_proj, ln2_g, ln2_b):
    bsz, seq, d = x.shape
    depth = w_in.shape[0]
    alpha = (2 * depth) ** 0.25
    t = bsz * seq
    h = x.reshape(t, d)
    for i in range(depth):
        q, k, v, sg_a, sg_b, o_b = _in_proj_s5(
            h, seq, w_in[i].astype(BF16), ssm_lambda_re[i], ssm_lambda_im[i], ssm_log_dt[i],
            ssm_b_re[i], ssm_b_im[i], ssm_c_re[i], ssm_c_im[i], ssm_d[i], ssm_glu_w[i], ssm_glu_b[i])
        o_a = _attention(q, k, v, bsz, seq)
        h = _merge_ln1(h, o_a, o_b, sg_a, sg_b, w_branch_a[i].astype(BF16), w_branch_b[i].astype(BF16),
                       w_out[i].astype(BF16), ln1_g[i], ln1_b[i], alpha)
        h = _ffn_ple_ln2(h, p[i].reshape(t, -1), ffn_w_gate[i].astype(BF16), ffn_w_up[i].astype(BF16),
                         ffn_w_down[i].astype(BF16), ple_w_gate[i].astype(BF16),
                         ple_w_proj[i].astype(BF16), ln2_g[i], ln2_b[i], alpha)
    return h.reshape(bsz, seq, d)
```
